```python
import jax
import jax.numpy as jnp
from jax import lax
import numpy as np

D_MODEL = 4096
BATCH = 4
SEQ = 2048
DEPTH = 4
DEC_BATCH = 8
DEC_SEQ = 1
PAST_LEN = 8192
PAGE_SIZE = 128

N_MIXERS = 2
N_RWKV_LAYERS = (DEPTH + 1) // 2
N_ATTN_LAYERS = DEPTH // 2
N_DENSE_LAYERS = (DEPTH + 1) // 2
N_MOE_LAYERS = DEPTH // 2
RWKV_HEAD = 64
RWKV_HEADS = D_MODEL // RWKV_HEAD
LORA_DECAY = max(32, int(round(1.8 * D_MODEL ** 0.5 / 32)) * 32)
LORA_AAA = max(32, int(round(1.8 * D_MODEL ** 0.5 / 32)) * 32)
LORA_MV = max(32, int(round(1.3 * D_MODEL ** 0.5 / 32)) * 32)
LORA_GATE = max(32, int(round(0.6 * D_MODEL ** 0.8 / 32)) * 32)
N_LERP = 6
GN_EPS = 64e-5
ATTN_HEAD = 128
ATTN_HEADS = D_MODEL // ATTN_HEAD
Q_BLOCK = 128
NEG_INF = -1e30
D_FF_DENSE = 256 * ((8 * D_MODEL // 3 + 255) // 256)
N_EXPERTS = 8
TOP_K = 2
D_FF_EXPERT = D_FF_DENSE // 2
NORM_EPS = 1e-6

kernel_name = "rwkv7_fox_hybrid_decode_step"


def rmsnorm(x, g):
    xf = x.astype(jnp.float32)
    y = xf * lax.rsqrt(jnp.mean(xf * xf, axis=-1, keepdims=True) + NORM_EPS)
    return (y * g.astype(jnp.float32)).astype(x.dtype)


def ada_modulation(c, w_ada, b_ada):
    m = jax.nn.silu(c) @ w_ada + b_ada
    shift, scale, gate = jnp.split(m[:, None, :], 3, axis=-1)
    return shift, scale, gate


def modulate(x, g, shift, scale):
    return rmsnorm(x, g) * (1 + scale) + shift


def rwkv7_mixer(h, shift_prev, S0, v_first, mu, w_rkv, w0, w1, w2, a0, a1, a2, vres,
                g1, g2, k_k, k_a, r_k, ln_w, ln_b, w_o):
    B, T, D = h.shape
    H, N = RWKV_HEADS, RWKV_HEAD
    f32 = jnp.float32
    h_prev = jnp.concatenate([shift_prev[:, None, :].astype(h.dtype), h[:, :-1]], axis=1)
    xx = h_prev - h
    xr, xw, xk, xv, xa, xg = (h + xx * mu[i] for i in range(N_LERP))
    r = xr @ w_rkv[0]
    k = xk @ w_rkv[1]
    v = xv @ w_rkv[2]
    w_log = -jax.nn.softplus(-(w0 + jnp.tanh(xw @ w1) @ w2)) - 0.5
    if vres is None:
        v_first = v
    else:
        v0, v1, v2 = vres
        v = v + (v_first - v) * jax.nn.sigmoid(v0 + (xv @ v1) @ v2)
    a = jax.nn.sigmoid(a0 + (xa @ a1) @ a2)
    g = jax.nn.sigmoid(xg @ g1) @ g2
    kk = (k * k_k).reshape(B, T, H, N).astype(f32)
    kk = kk / jnp.maximum(jnp.linalg.norm(kk, axis=-1, keepdims=True), 1e-12)
    k = k * (1 + (a - 1) * k_a)

    def th(t):
        return t.reshape(B, T, H, N).astype(f32)

    rh, kh, vh, ah = th(r), th(k), th(v), th(a)
    decay = jnp.exp(-jnp.exp(th(w_log)))
    xs = tuple(jnp.moveaxis(t, 1, 0) for t in (rh, decay, kh, vh, -kk, kk * ah))

    def step(S, inp):
        r_t, d_t, k_t, v_t, a_t, b_t = inp
        sa = jnp.einsum("bhvk,bhk->bhv", S, a_t)
        S = S * d_t[:, :, None, :] + sa[..., None] * b_t[:, :, None, :] + v_t[..., None] * k_t[:, :, None, :]
        return S, jnp.einsum("bhvk,bhk->bhv", S, r_t)

    S_T, ys = lax.scan(step, S0.astype(f32), xs)
    y = jnp.moveaxis(ys, 0, 1)
    mean = jnp.mean(y, axis=-1, keepdims=True)
    yc = y - mean
    var = jnp.mean(yc * yc, axis=-1, keepdims=True)
    yn = (yc * lax.rsqrt(var + GN_EPS)).reshape(B, T, D) * ln_w.astype(f32) + ln_b.astype(f32)
    bonus = (jnp.sum(rh * kh * r_k.astype(f32), axis=-1, keepdims=True) * vh).reshape(B, T, D)
    out = ((yn + bonus).astype(h.dtype) * g) @ w_o
    return out, v_first, h[:, -1], S_T.astype(h.dtype)


def fox_project(h, w_qkvf, b_f, q_g, k_g):
    B, T, _ = h.shape
    proj = h @ w_qkvf
    q = rmsnorm(proj[..., :D_MODEL].reshape(B, T, ATTN_HEADS, ATTN_HEAD), q_g)
    k = rmsnorm(proj[..., D_MODEL:2 * D_MODEL].reshape(B, T, ATTN_HEADS, ATTN_HEAD), k_g)
    v = proj[..., 2 * D_MODEL:3 * D_MODEL].reshape(B, T, ATTN_HEADS, ATTN_HEAD)
    logf = jax.nn.log_sigmoid((proj[..., 3 * D_MODEL:] + b_f).astype(jnp.float32))
    return q, k, v, logf


def fox_attention(q, k, v, logf, k_past=None, v_past=None, logf_past=None):
    T = q.shape[1]
    if k_past is not None:
        k = jnp.concatenate([k_past.astype(k.dtype), k], axis=1)
        v = jnp.concatenate([v_past.astype(v.dtype), v], axis=1)
        logf = jnp.concatenate([logf_past.astype(jnp.float32), logf], axis=1)
    P = k.shape[1] - T
    F = jnp.transpose(jnp.cumsum(logf, axis=1), (0, 2, 1))
    scale = ATTN_HEAD ** -0.5
    outs = []
    for q0 in range(0, T, Q_BLOCK):
        q1 = min(T, q0 + Q_BLOCK)
        kend = P + q1
        s = jnp.einsum("bqhd,bshd->bhqs", q[:, q0:q1], k[:, :kend]).astype(jnp.float32) * scale
        s = s + F[:, :, P + q0:P + q1, None] - F[:, :, None, :kend]
        q_pos = jnp.arange(P + q0, P + q1)
        k_pos = jnp.arange(kend)
        s = jnp.where(k_pos[None, :] <= q_pos[:, None], s, NEG_INF)
        p = jax.nn.softmax(s, axis=-1).astype(v.dtype)
        outs.append(jnp.einsum("bhqs,bshd->bqhd", p, v[:, :kend]))
    return jnp.concatenate(outs, axis=1)


def swiglu(h, w_gate, w_up, w_down):
    return (jax.nn.silu(h @ w_gate) * (h @ w_up)) @ w_down


def moe_swiglu(h, w_router, w_gate, w_up, w_down):
    probs = jax.nn.softmax((h @ w_router).astype(jnp.float32), axis=-1)
    top_p, top_i = lax.top_k(probs, TOP_K)
    top_p = top_p / jnp.sum(top_p, axis=-1, keepdims=True)
    gates = jnp.sum(jax.nn.one_hot(top_i, N_EXPERTS, dtype=jnp.float32) * top_p[..., None], axis=-2)
    gates = gates.astype(h.dtype)
    out = jnp.zeros_like(h)
    for e in range(N_EXPERTS):
        out = out + gates[..., e:e + 1] * swiglu(h, w_gate[e], w_up[e], w_down[e])
    return out


def setup_inputs(seed: int = 0) -> dict:
    key = jax.random.key(seed)
    ks = iter(jax.random.split(key, 64))

    def nrm(shape, scale=1.0):
        return jax.random.normal(next(ks), shape, jnp.float32) * scale

    def unif(shape, lo, hi):
        return jax.random.uniform(next(ks), shape, jnp.float32, lo, hi)

    D, H, N = D_MODEL, RWKV_HEADS, RWKV_HEAD
    HA, HD = ATTN_HEADS, ATTN_HEAD
    NR, NA = N_RWKV_LAYERS, N_ATTN_LAYERS
    n_pages = PAST_LEN // PAGE_SIZE
    used = DEC_BATCH * n_pages
    n_pool = used + (used + 3) // 4
    perm = jax.random.permutation(next(ks), n_pool)
    page_table = perm[:used].reshape(DEC_BATCH, n_pages).astype(jnp.int32)
    inv = D ** -0.5
    return {
        "x_prompt": nrm((BATCH, SEQ, D)),
        "x_sample": nrm((DEC_BATCH, DEC_SEQ, D)),
        "c_prompt": nrm((BATCH, D)),
        "c_sample": nrm((DEC_BATCH, D)),
        "cache_k": nrm((NA, n_pool, PAGE_SIZE, HA, HD)),
        "cache_v": nrm((NA, n_pool, PAGE_SIZE, HA, HD)),
        "cache_logf": jax.nn.log_sigmoid(3.0 + nrm((NA, n_pool, PAGE_SIZE, HA))),
        "page_table": page_table,
        "state_shift": nrm((NR, DEC_BATCH, D)),
        "state_wkv": nrm((NR, DEC_BATCH, H, N, N), 0.1),
        "ada_w": nrm((DEPTH, 2, D, 3 * D), 0.5 * inv),
        "ada_b": nrm((DEPTH, 2, 3 * D), 0.02),
        "norm_g": 1.0 + nrm((DEPTH, 2, D), 0.02),
        "rwkv_mu": unif((NR, N_LERP, D), 0.0, 1.0),
        "rwkv_w_rkv": nrm((NR, 3, D, D), inv),
        "rwkv_w0": unif((NR, D), -6.5, -1.5),
        "rwkv_w1": nrm((NR, D, LORA_DECAY), inv),
        "rwkv_w2": nrm((NR, LORA_DECAY, D), 0.1 * LORA_DECAY ** -0.5),
        "rwkv_a0": unif((NR, D), -0.5, 0.5),
        "rwkv_a1": nrm((NR, D, LORA_AAA), inv),
        "rwkv_a2": nrm((NR, LORA_AAA, D), 0.5 * LORA_AAA ** -0.5),
        "rwkv_v0": 1.0 + nrm((NR - 1, D), 0.1),
        "rwkv_v1": nrm((NR - 1, D, LORA_MV), inv),
        "rwkv_v2": nrm((NR - 1, LORA_MV, D), 0.5 * LORA_MV ** -0.5),
        "rwkv_g1": nrm((NR, D, LORA_GATE), inv),
        "rwkv_g2": nrm((NR, LORA_GATE, D), LORA_GATE ** -0.5),
        "rwkv_k_k": 0.85 + nrm((NR, D), 0.02),
        "rwkv_k_a": 1.0 + nrm((NR, D), 0.02),
        "rwkv_r_k": nrm((NR, H, N), 0.5),
        "rwkv_ln_w": 1.0 + nrm((NR, D), 0.02),
        "rwkv_ln_b": nrm((NR, D), 0.02),
        "rwkv_w_o": nrm((NR, D, D), inv),
        "attn_w_qkvf": nrm((NA, D, 3 * D + HA), inv),
        "attn_b_f": 3.0 + nrm((NA, HA), 0.5),
        "attn_q_g": 1.0 + nrm((NA, HD), 0.02),
        "attn_k_g": 1.0 + nrm((NA, HD), 0.02),
        "attn_w_o": nrm((NA, D, D), inv),
        "ffn_w_gate": nrm((N_DENSE_LAYERS, D, D_FF_DENSE), inv),
        "ffn_w_up": nrm((N_DENSE_LAYERS, D, D_FF_DENSE), inv),
        "ffn_w_down": nrm((N_DENSE_LAYERS, D_FF_DENSE, D), D_FF_DENSE ** -0.5),
        "moe_w_router": nrm((N_MOE_LAYERS, D, N_EXPERTS), inv),
        "moe_w_gate": nrm((N_MOE_LAYERS, N_EXPERTS, D, D_FF_EXPERT), inv),
        "moe_w_up": nrm((N_MOE_LAYERS, N_EXPERTS, D, D_FF_EXPERT), inv),
        "moe_w_down": nrm((N_MOE_LAYERS, N_EXPERTS, D_FF_EXPERT, D), D_FF_EXPERT ** -0.5),
    }


def reference(x_prompt, x_sample, c_prompt, c_sample, cache_k, cache_v, cache_logf, page_table,
              state_shift, state_wkv, ada_w, ada_b, norm_g, rwkv_mu, rwkv_w_rkv, rwkv_w0, rwkv_w1,
              rwkv_w2, rwkv_a0, rwkv_a1, rwkv_a2, rwkv_v0, rwkv_v1, rwkv_v2, rwkv_g1, rwkv_g2,
              rwkv_k_k, rwkv_k_a, rwkv_r_k, rwkv_ln_w, rwkv_ln_b, rwkv_w_o, attn_w_qkvf, attn_b_f,
              attn_q_g, attn_k_g, attn_w_o, ffn_w_gate, ffn_w_up, ffn_w_down, moe_w_router,
              moe_w_gate, moe_w_up, moe_w_down):

    def trunk(x, c, shift_in, wkv_in, paged):
        B, T, _ = x.shape
        v_first = None
        shifts, wkvs, ks, vs, lfs = [], [], [], [], []
        for i in range(DEPTH):
            j = i // N_MIXERS
            shift, scale, gate = ada_modulation(c, ada_w[i, 0], ada_b[i, 0])
            h = modulate(x, norm_g[i, 0], shift, scale)
            if i % N_MIXERS == 0:
                s_prev = jnp.zeros((B, D_MODEL), x.dtype) if shift_in is None else shift_in[j]
                S0 = (jnp.zeros((B, RWKV_HEADS, RWKV_HEAD, RWKV_HEAD), x.dtype)
                      if wkv_in is None else wkv_in[j])
                vres = None if j == 0 else (rwkv_v0[j - 1], rwkv_v1[j - 1], rwkv_v2[j - 1])
                out, v_first, last_h, S_T = rwkv7_mixer(
                    h, s_prev, S0, v_first, rwkv_mu[j], rwkv_w_rkv[j], rwkv_w0[j], rwkv_w1[j],
                    rwkv_w2[j], rwkv_a0[j], rwkv_a1[j], rwkv_a2[j], vres, rwkv_g1[j], rwkv_g2[j],
                    rwkv_k_k[j], rwkv_k_a[j], rwkv_r_k[j], rwkv_ln_w[j], rwkv_ln_b[j], rwkv_w_o[j])
                shifts.append(last_h)
                wkvs.append(S_T)
            else:
                q, k, v, logf = fox_project(h, attn_w_qkvf[j], attn_b_f[j], attn_q_g[j], attn_k_g[j])
                if paged:
                    k_past = cache_k[j, page_table].reshape(B, -1, ATTN_HEADS, ATTN_HEAD)
                    v_past = cache_v[j, page_table].reshape(B, -1, ATTN_HEADS, ATTN_HEAD)
                    lf_past = cache_logf[j, page_table].reshape(B, -1, ATTN_HEADS)
                    o = fox_attention(q, k, v, logf, k_past, v_past, lf_past)
                else:
                    o = fox_attention(q, k, v, logf)
                out = o.reshape(B, T, D_MODEL) @ attn_w_o[j]
                ks.append(k)
                vs.append(v)
                lfs.append(logf.astype(x.dtype))
            x = x + gate * out
            shift, scale, gate = ada_modulation(c, ada_w[i, 1], ada_b[i, 1])
            h = modulate(x, norm_g[i, 1], shift, scale)
            if i % 2 == 0:
                out = swiglu(h, ffn_w_gate[i // 2], ffn_w_up[i // 2], ffn_w_down[i // 2])
            else:
                out = moe_swiglu(h, moe_w_router[i // 2], moe_w_gate[i // 2], moe_w_up[i // 2],
                                 moe_w_down[i // 2])
            x = x + gate * out
        return (x, jnp.stack(ks), jnp.stack(vs), jnp.stack(lfs), jnp.stack(shifts), jnp.stack(wkvs))

    y_prompt, new_k_prompt, new_v_prompt, new_logf_prompt, new_shift_prompt, new_wkv_prompt = trunk(
        x_prompt, c_prompt, None, None, False)
    y_sample, new_k_sample, new_v_sample, new_logf_sample, new_shift_sample, new_wkv_sample = trunk(
        x_sample, c_sample, state_shift, state_wkv, True)
    return (y_prompt, y_sample, new_k_prompt, new_v_prompt, new_logf_prompt,
            new_k_sample, new_v_sample, new_logf_sample,
            new_shift_prompt, new_wkv_prompt, new_shift_sample, new_wkv_sample)
```

```python
import functools
import math

import jax
import jax.numpy as jnp
from jax import lax
from jax.experimental import pallas as pl
from jax.experimental.pallas import tpu as pltpu

F32 = jnp.float32
BF16 = jnp.bfloat16

NORM_EPS = 1e-6
GN_EPS = 64e-5
NEG_INF = -1e30
TOP_K = 2
N_LERP = 6

LANE = 128
SUBLANE = 8
VMEM_BYTES = 64 * 1024 * 1024
VMEM_CAP = VMEM_BYTES - 6 * 1024 * 1024
MIB = 1024 * 1024

HI = lax.Precision.HIGHEST

SCAN_CHUNK = 64
STEP_MXU_ROUND = True


def _params(sem, est_bytes):
    limit = min(max(int(est_bytes) + 4 * MIB, 32 * MIB), VMEM_CAP)
    return pltpu.CompilerParams(dimension_semantics=sem, vmem_limit_bytes=limit)


def _nbytes(shape, dtype):
    return math.prod(shape) * jnp.dtype(dtype).itemsize


def _div_tile(n, pref):
    if n <= pref:
        return n
    t = pref
    while n % t:
        t //= 2
    return t


def _dot(a, b):
    return jnp.dot(a.astype(BF16), b.astype(BF16), preferred_element_type=F32)


def _dot_nt(a, b):
    return lax.dot_general(a.astype(BF16), b.astype(BF16), (((1,), (1,)), ((), ())),
                           preferred_element_type=F32)


def _dot_tn(a, b):
    return lax.dot_general(a.astype(BF16), b.astype(BF16), (((0,), (0,)), ((), ())),
                           preferred_element_type=F32)


def _softplus(z):
    return jnp.maximum(z, 0.0) + jnp.log1p(jnp.exp(-jnp.abs(z)))


def _rms_mod(x, g, shift, scale):
    y = x * lax.rsqrt(jnp.mean(x * x, axis=-1, keepdims=True) + NORM_EPS)
    return (y * g) * (1.0 + scale) + shift


def _ada_body(c_ref, w_ref, b_ref, o_ref):
    c = c_ref[...]
    a = c * jax.nn.sigmoid(c)
    o_ref[0] = _dot(a, w_ref[0]) + b_ref[0]


def _ada_mods(c_all, ada_w, ada_b):
    S, D, N = ada_w.shape
    Rp = c_all.shape[0]
    tn = _div_tile(N, 512)
    est = 2 * (_nbytes((D, tn), F32) + _nbytes((Rp, D), F32) + 2 * _nbytes((Rp, tn), F32))
    return pl.pallas_call(
        _ada_body,
        grid=(S, N // tn),
        in_specs=[pl.BlockSpec((Rp, D), lambda s, j: (0, 0)),
                  pl.BlockSpec((1, D, tn), lambda s, j: (s, 0, j)),
                  pl.BlockSpec((1, 1, tn), lambda s, j: (s, 0, j))],
        out_specs=pl.BlockSpec((1, Rp, tn), lambda s, j: (s, 0, j)),
        out_shape=jax.ShapeDtypeStruct((S, Rp, N), F32),
        compiler_params=_params(("parallel", "parallel"), est),
        name="ada_mods",
    )(c_all, ada_w, ada_b)


def _mod_spec(mod, R, tr):
    D = mod.shape[-1]
    if mod.shape[1] == 1:
        return pl.BlockSpec((1, 1, D), lambda b, r: (b, 0, 0))
    assert mod.shape[1] == R
    return pl.BlockSpec((1, tr, D), lambda b, r: (b, r, 0))


def _prep_plain_body(x_ref, g_ref, sh_ref, sc_ref, h_ref):
    h_ref[0] = _rms_mod(x_ref[0], g_ref[...], sh_ref[0], sc_ref[0]).astype(h_ref.dtype)


def _prep_plain(x, g, shift, scale):
    Bx, R, D = x.shape
    tr = _div_tile(R, 256)
    est = 2 * (_nbytes((tr, D), F32) * 3 + _nbytes((tr, D), BF16))
    return pl.pallas_call(
        _prep_plain_body,
        grid=(Bx, R // tr),
        in_specs=[pl.BlockSpec((1, tr, D), lambda b, r: (b, r, 0)),
                  pl.BlockSpec((1, D), lambda b, r: (0, 0)),
                  _mod_spec(shift, R, tr), _mod_spec(scale, R, tr)],
        out_specs=pl.BlockSpec((1, tr, D), lambda b, r: (b, r, 0)),
        out_shape=jax.ShapeDtypeStruct((Bx, R, D), BF16),
        compiler_params=_params(("parallel", "parallel"), est),
        name="prep_plain",
    )(x, g, shift, scale)


def _prep_moe_body(n_exp, x_ref, g_ref, sh_ref, sc_ref, wr_ref, h_ref, gates_ref, sel_ref):
    h = _rms_mod(x_ref[0], g_ref[...], sh_ref[0], sc_ref[0])
    h_ref[0] = h.astype(h_ref.dtype)
    logits = _dot(h, wr_ref[...])
    lane = lax.broadcasted_iota(jnp.int32, logits.shape, 1)
    logits = jnp.where(lane < n_exp, logits, NEG_INF)
    e = jnp.exp(logits - jnp.max(logits, axis=-1, keepdims=True))
    p = e / jnp.sum(e, axis=-1, keepdims=True)
    p = jnp.where(lane < n_exp, p, -1.0)
    lane_f = lane.astype(F32)
    m1 = jnp.max(p, axis=-1, keepdims=True)
    i1 = jnp.min(jnp.where(p == m1, lane_f, float(LANE)), axis=-1, keepdims=True)
    p2 = jnp.where(lane_f == i1, -1.0, p)
    m2 = jnp.max(p2, axis=-1, keepdims=True)
    i2 = jnp.min(jnp.where(p2 == m2, lane_f, float(LANE)), axis=-1, keepdims=True)
    tot = m1 + m2
    gates_ref[0] = jnp.where(lane_f == i1, m1 / tot, jnp.where(lane_f == i2, m2 / tot, 0.0))
    sel_ref[0] = jnp.where((lane_f == i1) | (lane_f == i2), 1.0, 0.0)


def _prep_moe(x, g, shift, scale, w_router_pad, n_exp, h_dtype):
    Bx, R, D = x.shape
    tr = _div_tile(R, 256)
    est = 2 * (_nbytes((tr, D), F32) * 4 + _nbytes((D, LANE), F32))
    return pl.pallas_call(
        functools.partial(_prep_moe_body, n_exp),
        grid=(Bx, R // tr),
        in_specs=[pl.BlockSpec((1, tr, D), lambda b, r: (b, r, 0)),
                  pl.BlockSpec((1, D), lambda b, r: (0, 0)),
                  _mod_spec(shift, R, tr), _mod_spec(scale, R, tr),
                  pl.BlockSpec((D, LANE), lambda b, r: (0, 0))],
        out_specs=[pl.BlockSpec((1, tr, D), lambda b, r: (b, r, 0)),
                   pl.BlockSpec((1, tr, LANE), lambda b, r: (b, r, 0)),
                   pl.BlockSpec((1, tr, LANE), lambda b, r: (b, r, 0))],
        out_shape=[jax.ShapeDtypeStruct((Bx, R, D), h_dtype),
                   jax.ShapeDtypeStruct((Bx, R, LANE), F32),
                   jax.ShapeDtypeStruct((Bx, R, LANE), F32)],
        compiler_params=_params(("parallel", "parallel"), est),
        name="prep_moe",
    )(x, g, shift, scale, w_router_pad)


def _prep_rwkv_body(seq, x_ref, g_ref, sh_ref, sc_ref, mu_ref, sp_ref, lerp_ref, hl_ref, carry_ref):
    h = _rms_mod(x_ref[0], g_ref[...], sh_ref[0], sc_ref[0])
    tr = h.shape[0]
    if seq:
        @pl.when(pl.program_id(1) == 0)
        def _():
            carry_ref[...] = sp_ref[0]
        row = lax.broadcasted_iota(jnp.int32, h.shape, 0)
        h_prev = jnp.where(row == 0, carry_ref[...], pltpu.roll(h, 1, 0))
        carry_ref[...] = h[tr - 1:tr, :]
        hl_ref[0] = h[tr - 1:tr, :]
    else:
        h_prev = sp_ref[0]
        hl_ref[0] = h
    xx = h_prev - h
    for i in range(N_LERP):
        lerp_ref[i, 0] = (h + xx * mu_ref[i:i + 1, :]).astype(lerp_ref.dtype)


def _prep_rwkv(x, g, shift, scale, mu, shift_prev, seq):
    Bx, R, D = x.shape
    tr = _div_tile(R, 128)
    est = 2 * (_nbytes((tr, D), F32) * 4 + _nbytes((N_LERP, tr, D), BF16)) + _nbytes((N_LERP, D), F32)
    if seq:
        sp_spec = pl.BlockSpec((1, 1, D), lambda b, r: (b, 0, 0))
        hl_spec = pl.BlockSpec((1, 1, D), lambda b, r: (b, 0, 0))
        hl_shape = jax.ShapeDtypeStruct((Bx, 1, D), F32)
    else:
        sp_spec = pl.BlockSpec((1, tr, D), lambda b, r: (b, r, 0))
        hl_spec = pl.BlockSpec((1, tr, D), lambda b, r: (b, r, 0))
        hl_shape = jax.ShapeDtypeStruct((Bx, R, D), F32)
    return pl.pallas_call(
        functools.partial(_prep_rwkv_body, seq),
        grid=(Bx, R // tr),
        in_specs=[pl.BlockSpec((1, tr, D), lambda b, r: (b, r, 0)),
                  pl.BlockSpec((1, D), lambda b, r: (0, 0)),
                  _mod_spec(shift, R, tr), _mod_spec(scale, R, tr),
                  pl.BlockSpec((N_LERP, D), lambda b, r: (0, 0)),
                  sp_spec],
        out_specs=[pl.BlockSpec((N_LERP, 1, tr, D), lambda b, r: (0, b, r, 0)), hl_spec],
        out_shape=[jax.ShapeDtypeStruct((N_LERP, Bx, R, D), BF16), hl_shape],
        scratch_shapes=[pltpu.VMEM((1, D), F32)],
        compiler_params=_params(("parallel", "arbitrary"), est),
        name="prep_rwkv",
    )(x, g, shift, scale, mu, shift_prev)


def _linear_body(n_w, n_ex, n_pf, epilogue, *refs):
    refs = refs[n_pf:]
    a_ref = refs[0]
    w_refs = refs[1:1 + n_w]
    ex_refs = refs[1 + n_w:1 + n_w + n_ex]
    out_refs = refs[1 + n_w + n_ex:]
    a = a_ref[...].reshape(a_ref.shape[-2:])
    accs = [_dot(a, w[...].reshape(w.shape[-2:])) for w in w_refs]
    exs = [e[...].reshape(e.shape[-2:]) for e in ex_refs]
    res = epilogue(*accs, *exs)
    if not isinstance(res, (tuple, list)):
        res = (res,)
    for o, v in zip(out_refs, res):
        o[...] = v.astype(o.dtype).reshape(o.shape)


def _linear(grid, a, a_spec, ws, w_spec, extras, outs, epilogue, sem, prefetch=None, name="linear"):
    def blk_bytes(spec, dtype):
        dims = [d.block_size if isinstance(d, pl.Element) else (1 if d is None else d)
                for d in spec.block_shape]
        return _nbytes(dims, dtype)

    est = 2 * blk_bytes(a_spec, a.dtype)
    est += sum(2 * blk_bytes(w_spec, w.dtype) + blk_bytes(w_spec, BF16) for w in ws)
    est += sum(2 * blk_bytes(s, e.dtype) for e, s in extras)
    est += sum(4 * blk_bytes(s, F32) for o, s in outs)
    body = functools.partial(_linear_body, len(ws), len(extras), 0 if prefetch is None else 1, epilogue)
    in_specs = [a_spec] + [w_spec] * len(ws) + [s for _, s in extras]
    out_specs = [s for _, s in outs]
    out_shape = [o for o, _ in outs]
    args = [a] + list(ws) + [e for e, _ in extras]
    if prefetch is None:
        call = pl.pallas_call(body, grid=grid, in_specs=in_specs, out_specs=out_specs,
                              out_shape=out_shape, compiler_params=_params(sem, est), name=name)
        res = call(*args)
    else:
        gs = pltpu.PrefetchScalarGridSpec(num_scalar_prefetch=1, grid=grid, in_specs=in_specs,
                                          out_specs=out_specs)
        call = pl.pallas_call(body, grid_spec=gs, out_shape=out_shape,
                              compiler_params=_params(sem, est), name=name)
        res = call(prefetch, *args)
    return res


def _row_tile(M):
    return _div_tile(M, 1024)


def _gate_spec_ij(gate, rows_per_batch, tm, tn):
    if gate.shape[1] == 1:
        return pl.BlockSpec((1, 1, tn), lambda i, j: ((i * tm) // rows_per_batch, 0, j))
    assert gate.shape[0] == 1 and gate.shape[1] == tm
    return pl.BlockSpec((1, tm, tn), lambda i, j: (0, 0, j))


def _proj(a, w, w_lead, *, a_lead=0, n_cols=None, epilogue=None, extras=(), out_dtype=F32, tn_pref=512,
          name="proj"):
    _, M, K = a.shape
    N = w.shape[2] if n_cols is None else n_cols
    tm = _row_tile(M)
    tn = _div_tile(N, tn_pref if M > 64 else 1024)
    ex = []
    for e in extras:
        if e.shape[0] == 1:
            ex.append((e, pl.BlockSpec((1, tn), lambda i, j: (0, j))))
        else:
            ex.append((e, pl.BlockSpec((tm, tn), lambda i, j: (i, j))))
    epi = epilogue if epilogue is not None else (lambda acc: acc)
    (out,) = _linear(
        (M // tm, N // tn), a,
        pl.BlockSpec((1, tm, K), lambda i, j: (a_lead, i, 0)),
        [w], pl.BlockSpec((1, K, tn), lambda i, j: (w_lead, 0, j)),
        ex,
        [(jax.ShapeDtypeStruct((M, N), out_dtype), pl.BlockSpec((tm, tn), lambda i, j: (i, j)))],
        epi, ("parallel", "arbitrary"), name=name)
    return out


def _proj_residual(a, w, w_lead, x, gate, rows_per_batch, *, k_blk=0, k_size=None, prev=None,
                   final=True, name="proj_res"):
    M = a.shape[1]
    K = a.shape[2] if k_size is None else k_size
    N = w.shape[2]
    tm = _row_tile(min(M, rows_per_batch) if rows_per_batch > 1 else M)
    tn = _div_tile(N, 256 if M > 64 else 512)
    tile = lambda arr: (arr, pl.BlockSpec((tm, tn), lambda i, j: (i, j)))
    ex = []
    if prev is not None:
        ex.append(tile(prev))
    if final:
        ex.append(tile(x))
        ex.append((gate, _gate_spec_ij(gate, rows_per_batch, tm, tn)))

    def epi(acc, *e):
        e = list(e)
        if prev is not None:
            acc = acc + e.pop(0)
        if final:
            xv, gv = e
            acc = xv + gv * acc
        return acc

    (out,) = _linear(
        (M // tm, N // tn), a,
        pl.BlockSpec((1, tm, K), lambda i, j: (0, i, k_blk)),
        [w], pl.BlockSpec((1, K, tn), lambda i, j: (w_lead, k_blk, j)),
        ex,
        [(jax.ShapeDtypeStruct((M, N), F32), pl.BlockSpec((tm, tn), lambda i, j: (i, j)))],
        epi, ("parallel", "arbitrary"), name=name)
    return out


def _scan_body(C, T, t_valid, has_vres, *refs):
    if has_vres:
        (r_ref, k_ref, v_ref, w_ref, a_ref, g_ref, vg_ref, vf_ref,
         kk_ref, ka_ref, rk_ref, lnw_ref, lnb_ref, s0_ref, z_ref, st_ref) = refs
    else:
        (r_ref, k_ref, v_ref, w_ref, a_ref, g_ref,
         kk_ref, ka_ref, rk_ref, lnw_ref, lnb_ref, s0_ref, z_ref, st_ref) = refs
    half = LANE // 2
    lane = lax.broadcasted_iota(jnp.int32, (1, LANE), 1)
    m0 = lane < half
    r2 = lax.broadcasted_iota(jnp.int32, (2 * C, 2 * C), 0) % C
    c2 = lax.broadcasted_iota(jnp.int32, (2 * C, 2 * C), 1) % C
    stril = r2 > c2
    tril = r2 >= c2
    tri_c = (lax.broadcasted_iota(jnp.int32, (C, C), 0)
             >= lax.broadcasted_iota(jnp.int32, (C, C), 1)).astype(F32)
    kk_row, ka_row, rk_row = kk_ref[...], ka_ref[...], rk_ref[...]
    lnw_row, lnb_row = lnw_ref[...], lnb_ref[...]
    n_neumann = max(1, int(math.log2(C)))
    inv_half = 1.0 / half

    def seg_sum(x):
        s0 = jnp.sum(jnp.where(m0, x, 0.0), axis=-1, keepdims=True)
        s1 = jnp.sum(jnp.where(m0, 0.0, x), axis=-1, keepdims=True)
        return jnp.where(m0, s0, s1)

    def stack(x):
        return jnp.concatenate([jnp.where(m0, x, 0.0), jnp.where(m0, 0.0, x)], axis=0)

    def chunk(c, S):
        sl = pl.ds(pl.multiple_of(c * C, C), C)
        r = r_ref[0, 0, sl, :]
        k = k_ref[0, 0, sl, :]
        v = v_ref[0, 0, sl, :]
        a = a_ref[0, sl, :]
        logd = -jnp.exp(-_softplus(-w_ref[0, sl, :]) - 0.5)
        if has_vres:
            v = v + (vf_ref[0, sl, :] - v) * vg_ref[0, sl, :]
        kkr = k * kk_row
        kk = kkr / jnp.maximum(jnp.sqrt(seg_sum(kkr * kkr)), 1e-12)
        k2 = k * (1.0 + (a - 1.0) * ka_row)
        if t_valid < T:
            valid = (c * C + lax.broadcasted_iota(jnp.int32, (C, 1), 0)) < t_valid
            logd = jnp.where(valid, logd, 0.0)
            kk = jnp.where(valid, kk, 0.0)
            k2 = jnp.where(valid, k2, 0.0)
            v = jnp.where(valid, v, 0.0)
        cum = jnp.dot(tri_c, logd, precision=HI, preferred_element_type=F32)
        eg = jnp.exp(cum)
        eng = jnp.exp(-cum)
        a_st = stack(-kk * jnp.exp(cum - logd))
        r_st = stack(r * eg)
        b_st = stack(kk * a * eng)
        k_st = stack(k2 * eng)
        v_st = stack(v)
        ar = jnp.concatenate([a_st, r_st], axis=0)
        xb = _dot_nt(ar, b_st)
        xk = _dot_nt(ar, k_st)
        p = jnp.where(stril, xb[:2 * C], 0.0)
        lak = jnp.where(stril, xk[:2 * C], 0.0)
        mrb = jnp.where(tril, xb[2 * C:], 0.0)
        mrk = jnp.where(tril, xk[2 * C:], 0.0)
        a_s = _dot_nt(ar, S)
        u = a_s[:2 * C] + _dot(lak, v_st)
        for i in range(n_neumann):
            u = u + _dot(p, u)
            if i + 1 < n_neumann:
                p = _dot(p, p)
        y_st = a_s[2 * C:] + _dot(mrb, u) + _dot(mrk, v_st)
        y = y_st[:C] + y_st[C:]
        eg_last = eg[C - 1:C, :]
        S_new = S * eg_last + _dot_tn(u, b_st * eg_last) + _dot_tn(v_st, k_st * eg_last)
        mean = seg_sum(y) * inv_half
        yc = y - mean
        var = seg_sum(yc * yc) * inv_half
        yn = yc * lax.rsqrt(var + GN_EPS) * lnw_row + lnb_row
        bonus = seg_sum(r * k2 * rk_row) * v
        z_ref[0, sl, :] = ((yn + bonus) * g_ref[0, sl, :]).astype(z_ref.dtype)
        return S_new

    S = lax.fori_loop(0, T // C, chunk, s0_ref[0, 0])
    st_ref[0, 0] = S


def _rwkv_scan(rkv, w_pre, a_gate, g, vres, vecs, s0_bd, C, t_valid):
    _, B, T, D = rkv.shape
    nb = D // LANE
    seq_spec = pl.BlockSpec((1, T, LANE), lambda b, h: (b, 0, h))
    rkv_spec = lambda i: pl.BlockSpec((1, 1, T, LANE), lambda b, h: (i, b, 0, h))
    vec_spec = pl.BlockSpec((1, LANE), lambda b, h: (0, h))
    st_spec = pl.BlockSpec((1, 1, LANE, LANE), lambda b, h: (b, h, 0, 0))
    has_vres = vres is not None
    n_seq = 6 + (2 if has_vres else 0)
    in_specs = [rkv_spec(0), rkv_spec(1), rkv_spec(2)] + [seq_spec] * (n_seq - 3) + [vec_spec] * 5 + [st_spec]
    args = [rkv, rkv, rkv, w_pre, a_gate, g] + (list(vres) if has_vres else []) + list(vecs) + [s0_bd]
    est = 2 * (n_seq * _nbytes((T, LANE), F32) + _nbytes((T, LANE), BF16) + 2 * _nbytes((LANE, LANE), F32))
    est += 64 * _nbytes((2 * C, LANE), F32)
    return pl.pallas_call(
        functools.partial(_scan_body, C, T, t_valid, has_vres),
        grid=(B, nb),
        in_specs=in_specs,
        out_specs=[seq_spec, st_spec],
        out_shape=[jax.ShapeDtypeStruct((B, T, D), BF16),
                   jax.ShapeDtypeStruct((B, nb, LANE, LANE), F32)],
        compiler_params=_params(("parallel", "parallel"), est),
        name="rwkv_scan",
    )(*args)


def _to_block_diag(s):
    B, H, N, _ = s.shape
    s = s.reshape(B, H // 2, 2, N, N)
    z = jnp.zeros_like(s[:, :, 0])
    top = jnp.concatenate([s[:, :, 0], z], axis=-1)
    bot = jnp.concatenate([z, s[:, :, 1]], axis=-1)
    return jnp.concatenate([top, bot], axis=-2)


def _from_block_diag(s_bd, N):
    B, nb = s_bd.shape[:2]
    return jnp.stack([s_bd[:, :, :N, :N], s_bd[:, :, N:, N:]], axis=2).reshape(B, 2 * nb, N, N)


def _rwkv_step_body(has_vres, mxu_round, *refs):
    if has_vres:
        (r_ref, k_ref, v_ref, w_ref, a_ref, g_ref, vg_ref, vf_ref,
         kk_ref, ka_ref, rk_ref, lnw_ref, lnb_ref, s0_ref, z_ref, st_ref) = refs
    else:
        (r_ref, k_ref, v_ref, w_ref, a_ref, g_ref,
         kk_ref, ka_ref, rk_ref, lnw_ref, lnb_ref, s0_ref, z_ref, st_ref) = refs
    r, k, v, a = r_ref[0], k_ref[0], v_ref[0], a_ref[0]
    S = s0_ref[0]
    N = S.shape[-1]
    eye = (lax.broadcasted_iota(jnp.int32, (N, N), 0) == lax.broadcasted_iota(jnp.int32, (N, N), 1))
    col = lambda x: jnp.sum(jnp.where(eye, x, 0.0), axis=-1, keepdims=True)
    row = lambda x: jnp.sum(jnp.where(eye, x, 0.0), axis=-2, keepdims=True)
    rnd = (lambda x: x.astype(BF16).astype(F32)) if mxu_round else (lambda x: x)
    decay = jnp.exp(-jnp.exp(-_softplus(-w_ref[0]) - 0.5))
    if has_vres:
        v = v + (vf_ref[0] - v) * vg_ref[0]
    kkr = k * kk_ref[...]
    kk = kkr / jnp.maximum(jnp.sqrt(jnp.sum(kkr * kkr, axis=-1, keepdims=True)), 1e-12)
    k2 = k * (1.0 + (a - 1.0) * ka_ref[...])
    sa = jnp.sum(rnd(S) * rnd(-kk), axis=-1, keepdims=True)
    S_new = S * decay + sa * (kk * a) + col(v) * k2
    st_ref[0] = S_new
    y = jnp.sum(rnd(S_new) * rnd(r), axis=-1, keepdims=True)
    mean = jnp.mean(y, axis=-2, keepdims=True)
    yc = y - mean
    var = jnp.mean(yc * yc, axis=-2, keepdims=True)
    yn = row(yc * lax.rsqrt(var + GN_EPS)) * lnw_ref[...] + lnb_ref[...]
    bonus = jnp.sum(r * k2 * rk_ref[...], axis=-1, keepdims=True) * v
    z_ref[0] = (yn + bonus) * g_ref[0]


def _rwkv_step(seqs, vecs, s0, mxu_round):
    Bd, H, N, _ = s0.shape
    row_spec = pl.BlockSpec((1, H, 1, N), lambda b: (b, 0, 0, 0))
    vec_spec = pl.BlockSpec((H, 1, N), lambda b: (0, 0, 0))
    st_spec = pl.BlockSpec((1, H, N, N), lambda b: (b, 0, 0, 0))
    est = 24 * _nbytes((H, N, LANE), F32)
    return pl.pallas_call(
        functools.partial(_rwkv_step_body, len(seqs) == 8, mxu_round),
        grid=(Bd,),
        in_specs=[row_spec] * len(seqs) + [vec_spec] * 5 + [st_spec],
        out_specs=[row_spec, st_spec],
        out_shape=[jax.ShapeDtypeStruct((Bd, H, 1, N), F32), jax.ShapeDtypeStruct((Bd, H, N, N), F32)],
        compiler_params=_params(("parallel",), est),
        name="rwkv_step",
    )(*seqs, *vecs, s0)


def _cumsum_body(tb, lf_ref, f_ref):
    T = lf_ref.shape[1]
    tri = (lax.broadcasted_iota(jnp.int32, (tb, tb), 0)
           >= lax.broadcasted_iota(jnp.int32, (tb, tb), 1)).astype(F32)

    def blk(i, carry):
        sl = pl.ds(pl.multiple_of(i * tb, tb), tb)
        f = jnp.dot(tri, lf_ref[0, sl, :], precision=HI, preferred_element_type=F32) + carry
        f_ref[0, sl, :] = f
        return f[tb - 1:tb, :]

    lax.fori_loop(0, T // tb, blk, jnp.zeros((1, lf_ref.shape[2]), F32))


def _cumsum_rows(lf):
    B, T, W = lf.shape
    tb = _div_tile(T, 256)
    return pl.pallas_call(
        functools.partial(_cumsum_body, tb),
        grid=(B,),
        in_specs=[pl.BlockSpec((1, T, W), lambda b: (b, 0, 0))],
        out_specs=pl.BlockSpec((1, T, W), lambda b: (b, 0, 0)),
        out_shape=jax.ShapeDtypeStruct((B, T, W), F32),
        compiler_params=_params(("parallel",), 4 * _nbytes((T, W), F32)),
        name="logf_cumsum",
    )(lf)


def _fox_prefill_body(tq, scale, q_ref, k_ref, v_ref, fr_ref, fc_ref, o_ref, kb_ref, vb_ref):
    T = q_ref.shape[2]
    kb_ref[...] = k_ref[0, 0].astype(BF16)
    vb_ref[...] = v_ref[0, 0].astype(BF16)
    row = lax.broadcasted_iota(jnp.int32, (tq, tq), 0)
    col = lax.broadcasted_iota(jnp.int32, (tq, tq), 1)

    def q_blk(qi, _):
        qs = pl.ds(pl.multiple_of(qi * tq, tq), tq)
        q = q_ref[0, 0, qs, :].astype(BF16)
        fq = fc_ref[0, 0, qs, :]

        def kv_blk(kj, carry):
            m, l, acc = carry
            ks = pl.ds(pl.multiple_of(kj * tq, tq), tq)
            s = _dot_nt(q, kb_ref[ks, :]) * scale + fq - fr_ref[0, 0, :, ks]
            s = jnp.where(col + kj * tq <= row + qi * tq, s, NEG_INF)
            m_new = jnp.maximum(m, jnp.max(s, axis=-1, keepdims=True))
            alpha = jnp.exp(m - m_new)
            p = jnp.exp(s - m_new)
            l = alpha * l + jnp.sum(p, axis=-1, keepdims=True)
            acc = alpha * acc + _dot(p, vb_ref[ks, :])
            return m_new, l, acc

        init = (jnp.full((tq, 1), NEG_INF, F32), jnp.zeros((tq, 1), F32), jnp.zeros((tq, LANE), F32))
        _, l, acc = lax.fori_loop(0, qi + 1, kv_blk, init)
        o_ref[0, qs, :] = (acc / l).astype(o_ref.dtype)
        return 0

    lax.fori_loop(0, T // tq, q_blk, 0)


def _fox_prefill(qkv, f_row, f_col, head_dim):
    _, B, T, D = qkv.shape
    H = D // head_dim
    assert head_dim == LANE
    tq = _div_tile(T, 256)
    spec = lambda i: pl.BlockSpec((1, 1, T, LANE), lambda b, h: (i, b, 0, h))
    est = 2 * (3 * _nbytes((T, LANE), F32) + _nbytes((T, LANE), F32) + 2 * _nbytes((T, LANE), BF16))
    return pl.pallas_call(
        functools.partial(_fox_prefill_body, tq, head_dim ** -0.5),
        grid=(B, H),
        in_specs=[spec(0), spec(1), spec(2),
                  pl.BlockSpec((1, 1, 1, T), lambda b, h: (b, h, 0, 0)),
                  pl.BlockSpec((1, 1, T, 1), lambda b, h: (b, h, 0, 0))],
        out_specs=pl.BlockSpec((1, T, LANE), lambda b, h: (b, 0, h)),
        out_shape=jax.ShapeDtypeStruct((B, T, D), BF16),
        scratch_shapes=[pltpu.VMEM((T, LANE), BF16), pltpu.VMEM((T, LANE), BF16)],
        compiler_params=_params(("parallel", "parallel"), est),
        name="fox_prefill",
    )(qkv, qkv, qkv, f_row, f_col)


def _fox_decode_body(n_heads, scale, pt_ref, q_ref, kn_ref, vn_ref, lfn_ref, kp_ref, vp_ref, lfp_ref,
                     o_ref, qm_ref, m_ref, l_ref, acc_ref, carry_ref):
    p_id = pl.program_id(1)
    n_pages = pl.num_programs(1)
    P = kp_ref.shape[0]
    n_grp = n_heads // SUBLANE
    sub = lax.broadcasted_iota(jnp.int32, (SUBLANE, LANE), 0)

    @pl.when(p_id == 0)
    def _():
        q = q_ref[0]
        for h in range(n_heads):
            qm_ref[h] = jnp.where(sub == h % SUBLANE, q[h:h + 1, :], 0.0)
        m_ref[...] = jnp.full(m_ref.shape, NEG_INF, F32)
        l_ref[...] = jnp.zeros(l_ref.shape, F32)
        acc_ref[...] = jnp.zeros(acc_ref.shape, F32)
        carry_ref[...] = lfn_ref[0]

    lf = lfp_ref[...]
    later = (lax.broadcasted_iota(jnp.int32, (P, P), 0)
             > lax.broadcasted_iota(jnp.int32, (P, P), 1)).astype(F32)
    bias = jnp.dot(lf, later, precision=HI, preferred_element_type=F32) + carry_ref[...]
    carry_ref[...] = carry_ref[...] + jnp.sum(lf, axis=-1, keepdims=True)

    s_grp = []
    for gi in range(n_grp):
        s = jnp.zeros((SUBLANE, P), F32)
        for hh in range(SUBLANE):
            h = gi * SUBLANE + hh
            s = s + _dot_nt(qm_ref[h], kp_ref[:, h * LANE:(h + 1) * LANE])
        s_grp.append(s)
    s = jnp.concatenate(s_grp, axis=0) * scale + bias
    m_new = jnp.maximum(m_ref[...], jnp.max(s, axis=-1, keepdims=True))
    alpha = jnp.exp(m_ref[...] - m_new)
    p = jnp.exp(s - m_new)
    l_ref[...] = alpha * l_ref[...] + jnp.sum(p, axis=-1, keepdims=True)
    m_ref[...] = m_new
    pv_grp = []
    for gi in range(n_grp):
        pg = p[gi * SUBLANE:(gi + 1) * SUBLANE, :]
        pv = jnp.zeros((SUBLANE, LANE), F32)
        for hh in range(SUBLANE):
            h = gi * SUBLANE + hh
            pv = pv + _dot(jnp.where(sub == hh, pg, 0.0), vp_ref[:, h * LANE:(h + 1) * LANE])
        pv_grp.append(pv)
    acc_ref[...] = alpha * acc_ref[...] + jnp.concatenate(pv_grp, axis=0)

    @pl.when(p_id == n_pages - 1)
    def _():
        s_new = jnp.sum(q_ref[0] * kn_ref[0], axis=-1, keepdims=True) * scale
        m_fin = jnp.maximum(m_ref[...], s_new)
        al = jnp.exp(m_ref[...] - m_fin)
        p_new = jnp.exp(s_new - m_fin)
        l_fin = al * l_ref[...] + p_new
        o_ref[0] = ((al * acc_ref[...] + p_new * vn_ref[0]) / l_fin).astype(o_ref.dtype)


def _fox_decode(q, k_new, v_new, lf_new, cache_k, cache_v, cache_lft, layer, page_table, head_dim):
    Bd, H, hd = q.shape
    assert hd == LANE and H % SUBLANE == 0
    P, D = cache_k.shape[2], cache_k.shape[3]
    n_pages = page_table.shape[1]
    row = lambda: pl.BlockSpec((1, H, hd), lambda b, p, pt: (b, 0, 0))
    page_idx = lambda b, p, pt: (layer, pt[b, n_pages - 1 - p], 0, 0)
    gs = pltpu.PrefetchScalarGridSpec(
        num_scalar_prefetch=1,
        grid=(Bd, n_pages),
        in_specs=[row(), row(), row(),
                  pl.BlockSpec((1, H, 1), lambda b, p, pt: (b, 0, 0)),
                  pl.BlockSpec((None, None, P, D), page_idx),
                  pl.BlockSpec((None, None, P, D), page_idx),
                  pl.BlockSpec((None, None, H, P), page_idx)],
        out_specs=pl.BlockSpec((1, H, hd), lambda b, p, pt: (b, 0, 0)),
        scratch_shapes=[pltpu.VMEM((H, SUBLANE, LANE), F32),
                        pltpu.VMEM((H, 1), F32), pltpu.VMEM((H, 1), F32),
                        pltpu.VMEM((H, LANE), F32), pltpu.VMEM((H, 1), F32)])
    est = 4 * _nbytes((P, D), F32) + 2 * MIB
    return pl.pallas_call(
        functools.partial(_fox_decode_body, H, head_dim ** -0.5),
        grid_spec=gs,
        out_shape=jax.ShapeDtypeStruct((Bd, H, hd), BF16),
        compiler_params=_params(("parallel", "arbitrary"), est),
        name="fox_decode",
    )(page_table, q, k_new, v_new, lf_new, cache_k, cache_v, cache_lft)


def _head_rms(acc, g_row, head_dim):
    tm, tn = acc.shape
    parts = []
    for h in range(tn // head_dim):
        blk = acc[:, h * head_dim:(h + 1) * head_dim]
        blk = blk * lax.rsqrt(jnp.mean(blk * blk, axis=-1, keepdims=True) + NORM_EPS)
        parts.append(blk * g_row)
    return parts[0] if len(parts) == 1 else jnp.concatenate(parts, axis=-1)


def _qkv_proj(h, w_qkvf, layer, qk_gain, head_dim):
    _, M, D = h.shape
    tm = _row_tile(M)
    tn = _div_tile(D, 512 if M > 64 else 1024)
    nj = D // tn

    def epi(acc, gq):
        return _head_rms(acc, gq, head_dim)

    (qk,) = _linear(
        (M // tm, 2 * nj), h,
        pl.BlockSpec((1, tm, D), lambda i, j: (0, i, 0)),
        [w_qkvf], pl.BlockSpec((1, D, tn), lambda i, j: (layer, 0, j)),
        [(qk_gain, pl.BlockSpec((1, 1, head_dim), lambda i, j: (j // nj, 0, 0)))],
        [(jax.ShapeDtypeStruct((2, M, D), F32),
          pl.BlockSpec((1, tm, tn), lambda i, j: (j // nj, i, j % nj)))],
        epi, ("parallel", "arbitrary"), name="qk_proj")
    (v,) = _linear(
        (M // tm, nj), h,
        pl.BlockSpec((1, tm, D), lambda i, j: (0, i, 0)),
        [w_qkvf], pl.BlockSpec((1, D, tn), lambda i, j: (layer, 0, 2 * nj + j)),
        [],
        [(jax.ShapeDtypeStruct((1, M, D), F32), pl.BlockSpec((1, tm, tn), lambda i, j: (0, i, j)))],
        lambda acc: acc, ("parallel", "arbitrary"), name="v_proj")
    return jnp.concatenate([qk, v], axis=0)


def _swiglu_epi(g, u):
    return (g * jax.nn.sigmoid(g)) * u


def _col_tiles(F, tn):
    assert F % LANE == 0 and tn % LANE == 0
    n = -(-F // tn)
    return n, (lambda j: pl.multiple_of(jnp.minimum(j * tn, F - tn), LANE))


def _ffn_up_dense(h, w_gate, w_up, layer):
    _, M, D = h.shape
    F = w_gate.shape[2]
    tm = _row_tile(M)
    tn = 256 if M > 64 else 512
    n_j, off = _col_tiles(F, tn)
    E = pl.Element
    (out,) = _linear(
        (M // tm, n_j), h,
        pl.BlockSpec((1, tm, D), lambda i, j: (0, i, 0)),
        [w_gate, w_up], pl.BlockSpec((None, E(D), E(tn)), lambda i, j: (layer, 0, off(j))),
        [],
        [(jax.ShapeDtypeStruct((M, F), BF16), pl.BlockSpec((E(tm), E(tn)), lambda i, j: (i * tm, off(j))))],
        _swiglu_epi, ("parallel", "arbitrary"), name="ffn_up")
    return out


def _moe_up_sorted(a_sorted, w_gate, w_up, layer_base, tile_expert, tm):
    _, Mp, D = a_sorted.shape
    F = w_gate.shape[2]
    tn = 512
    n_j, off = _col_tiles(F, tn)
    E = pl.Element
    (out,) = _linear(
        (n_j, Mp // tm), a_sorted,
        pl.BlockSpec((1, tm, D), lambda j, s, te: (0, s, 0)),
        [w_gate, w_up],
        pl.BlockSpec((None, E(D), E(tn)), lambda j, s, te: (layer_base + te[s], 0, off(j))),
        [],
        [(jax.ShapeDtypeStruct((Mp, F), BF16),
          pl.BlockSpec((E(tm), E(tn)), lambda j, s, te: (s * tm, off(j))))],
        _swiglu_epi, ("parallel", "arbitrary"), prefetch=tile_expert, name="moe_up")
    return out


def _moe_down_sorted(a_sorted, w_down, layer_base, tile_expert, pair_gate, tm):
    _, Mp, F = a_sorted.shape
    D = w_down.shape[2]
    tn = _div_tile(D, 512)
    (out,) = _linear(
        (D // tn, Mp // tm), a_sorted,
        pl.BlockSpec((1, tm, F), lambda j, s, te: (0, s, 0)),
        [w_down], pl.BlockSpec((1, F, tn), lambda j, s, te: (layer_base + te[s], 0, j)),
        [(pair_gate, pl.BlockSpec((tm, 1), lambda j, s, te: (s, 0)))],
        [(jax.ShapeDtypeStruct((Mp, D), F32), pl.BlockSpec((tm, tn), lambda j, s, te: (s, j)))],
        lambda acc, pg: acc * pg, ("parallel", "arbitrary"), prefetch=tile_expert, name="moe_down")
    return out


def _gather_rows_body(n_rows, *refs):
    rows = refs[1:1 + n_rows]
    out_ref = refs[1 + n_rows]
    buf_ref = refs[2 + n_rows]
    for r in range(n_rows):
        buf_ref[r:r + 1, :] = rows[r][0]
    out_ref[...] = buf_ref[...].astype(out_ref.dtype)


def _gather_rows(src, idx, n_out, group=16):
    M, _, D = src.shape
    row_spec = lambda r: pl.BlockSpec((1, 1, D), lambda s, ix: (ix[s * group + r], 0, 0))
    gs = pltpu.PrefetchScalarGridSpec(
        num_scalar_prefetch=1, grid=(n_out // group,),
        in_specs=[row_spec(r) for r in range(group)],
        out_specs=pl.BlockSpec((group, D), lambda s, ix: (s, 0)),
        scratch_shapes=[pltpu.VMEM((group, D), F32)])
    return pl.pallas_call(
        functools.partial(_gather_rows_body, group),
        grid_spec=gs,
        out_shape=jax.ShapeDtypeStruct((n_out, D), BF16),
        compiler_params=_params(("arbitrary",), 8 * group * _nbytes((SUBLANE, D), F32)),
        name="moe_gather",
    )(idx, *([src] * group))


def _combine_body(n_tok, *refs):
    rows = refs[1:1 + 2 * n_tok]
    x_ref, gate_ref, out_ref, buf_ref = refs[1 + 2 * n_tok:]
    for t in range(n_tok):
        buf_ref[t:t + 1, :] = rows[2 * t][0] + rows[2 * t + 1][0]
    out_ref[0] = x_ref[0] + gate_ref[0] * buf_ref[...]


def _moe_combine(y_sorted, pos, x, gate, group=8):
    Bx, R, D = x.shape
    steps_per_batch = R // group
    row_spec = lambda r: pl.BlockSpec(
        (1, 1, D), lambda b, s, ps: (ps[(b * steps_per_batch + s) * 2 * group + r], 0, 0))
    gs = pltpu.PrefetchScalarGridSpec(
        num_scalar_prefetch=1, grid=(Bx, steps_per_batch),
        in_specs=[row_spec(r) for r in range(2 * group)]
        + [pl.BlockSpec((1, group, D), lambda b, s, ps: (b, s, 0)),
           pl.BlockSpec((1, 1, D), lambda b, s, ps: (b, 0, 0))],
        out_specs=pl.BlockSpec((1, group, D), lambda b, s, ps: (b, s, 0)),
        scratch_shapes=[pltpu.VMEM((group, D), F32)])
    return pl.pallas_call(
        functools.partial(_combine_body, group),
        grid_spec=gs,
        out_shape=jax.ShapeDtypeStruct((Bx, R, D), F32),
        compiler_params=_params(("arbitrary", "arbitrary"), 8 * 2 * group * _nbytes((SUBLANE, D), F32)),
        name="moe_combine",
    )(pos, *([y_sorted] * (2 * group)), x, gate)


def _moe_up_all(h, w_gate, w_up, layer_base, n_exp):
    _, R, D = h.shape
    F = w_gate.shape[2]
    tn = 512
    n_j, off = _col_tiles(F, tn)
    E = pl.Element
    (out,) = _linear(
        (n_exp, n_j), h,
        pl.BlockSpec((1, R, D), lambda e, j: (0, 0, 0)),
        [w_gate, w_up], pl.BlockSpec((None, E(D), E(tn)), lambda e, j: (layer_base + e, 0, off(j))),
        [],
        [(jax.ShapeDtypeStruct((n_exp, R, F), BF16),
          pl.BlockSpec((None, E(R), E(tn)), lambda e, j: (e, 0, off(j))))],
        _swiglu_epi, ("parallel", "arbitrary"), name="moe_up_all")
    return out


def _moe_down_all_body(a_ref, w_ref, ge_ref, x_ref, gate_ref, o_ref, acc_ref):
    e = pl.program_id(1)

    @pl.when(e == 0)
    def _():
        acc_ref[...] = jnp.zeros(acc_ref.shape, F32)

    acc_ref[...] += ge_ref[0] * _dot(a_ref[0], w_ref[0])

    @pl.when(e == pl.num_programs(1) - 1)
    def _():
        o_ref[0] = x_ref[0] + gate_ref[0] * acc_ref[...]


def _moe_down_all(a, w_down, layer_base, gates_t, x, gate):
    n_exp, R, F = a.shape
    D = w_down.shape[2]
    tn = _div_tile(D, 512)
    est = 2 * (_nbytes((F, tn), F32) + _nbytes((R, F), BF16)) + _nbytes((F, tn), BF16) + 8 * _nbytes((R, tn), F32)
    return pl.pallas_call(
        _moe_down_all_body,
        grid=(D // tn, n_exp),
        in_specs=[pl.BlockSpec((1, R, F), lambda j, e: (e, 0, 0)),
                  pl.BlockSpec((1, F, tn), lambda j, e: (layer_base + e, 0, j)),
                  pl.BlockSpec((1, R, 1), lambda j, e: (e, 0, 0)),
                  pl.BlockSpec((1, R, tn), lambda j, e: (0, 0, j)),
                  pl.BlockSpec((1, R, tn), lambda j, e: (0, 0, j))],
        out_specs=pl.BlockSpec((1, R, tn), lambda j, e: (0, 0, j)),
        out_shape=jax.ShapeDtypeStruct((1, R, D), F32),
        scratch_shapes=[pltpu.VMEM((R, tn), F32)],
        compiler_params=_params(("parallel", "arbitrary"), est),
        name="moe_down_all",
    )(a, w_down, gates_t, x, gate)


def _route_metadata(gates, selmask, n_exp, tm):
    M = gates.shape[0]
    n_tiles = (M * TOP_K) // tm + n_exp
    sel = selmask > 0.5
    seli = sel.astype(jnp.int32)
    rank = jnp.cumsum(seli, axis=0) - seli
    counts = jnp.sum(seli, axis=0)
    tiles_per = (counts + tm - 1) // tm
    tile_start = jnp.cumsum(tiles_per) - tiles_per
    pos = tile_start[None, :] * tm + rank
    tile_expert = jnp.clip(
        jnp.searchsorted(jnp.cumsum(tiles_per), jnp.arange(n_tiles, dtype=jnp.int32), side="right"),
        0, n_exp - 1).astype(jnp.int32)
    flat_pos = jnp.where(sel, pos, n_tiles * tm).reshape(-1)
    tok = jnp.broadcast_to(jnp.arange(M, dtype=jnp.int32)[:, None], (M, n_exp)).reshape(-1)
    row_src = jnp.zeros((n_tiles * tm,), jnp.int32).at[flat_pos].set(tok, mode="drop")
    row_gate = jnp.zeros((n_tiles * tm,), F32).at[flat_pos].set(gates.reshape(-1), mode="drop")
    order = jnp.argsort(jnp.where(sel, 0, 1), axis=1, stable=True)[:, :TOP_K]
    tok_pos = jnp.take_along_axis(pos, order, axis=1).astype(jnp.int32).reshape(-1)
    return row_src, tile_expert, row_gate.reshape(-1, 1), tok_pos


def _trunk(x, mods, rows_per_batch, seq, P, shift_in, wkv_in, paged):
    Bx, R, D = x.shape
    M = Bx * R
    depth = P["norm_g"].shape[0]
    n_rwkv_heads, n_state = P["state_dims"]
    head_dim = P["attn_q_g"].shape[-1]
    n_attn_heads = D // head_dim
    n_exp = P["moe_w_router"].shape[-1]
    lo_w, lo_a, lo_g = P["rwkv_w1"].shape[-1], P["rwkv_a1"].shape[-1], P["rwkv_g1"].shape[-1]
    lo_v = P["rwkv_v1"].shape[-1]
    n_seqs = Bx if seq else R
    T = R if seq else 1
    flat = lambda t: t.reshape(1, M, t.shape[-1])

    shifts, wkvs, ks, vs, lfs = [], [], [], [], []
    v_first = None
    for i in range(depth):
        j = i // 2
        shift, scale, gate = mods[(i, 0)]
        g_row = P["norm_g"][i, 0][None, :]
        if i % 2 == 0:
            mu = P["rwkv_mu"][j][jnp.array([0, 2, 3, 1, 4, 5])]
            if seq:
                sp = jnp.zeros((Bx, 1, D), F32) if shift_in is None else shift_in[j][:, None, :]
            else:
                sp = shift_in[j][None]
            lerps, h_keep = _prep_rwkv(x, g_row, shift, scale, mu, sp, seq)
            lerps = lerps.reshape(N_LERP, M, D)
            shifts.append(h_keep.reshape(n_seqs, D))
            n_l = P["rwkv_w_rkv"].shape[0]
            w_rkv = P["rwkv_w_rkv"].reshape(n_l * 3, D, D)
            tm = _row_tile(M)
            tn = _div_tile(D, 512 if M > 64 else 1024)
            (rkv,) = _linear(
                (3, M // tm, D // tn), lerps,
                pl.BlockSpec((1, tm, D), lambda s, a, b: (s, a, 0)),
                [w_rkv], pl.BlockSpec((1, D, tn), lambda s, a, b: (3 * j + s, 0, b)),
                [],
                [(jax.ShapeDtypeStruct((3, M, D), F32), pl.BlockSpec((1, tm, tn), lambda s, a, b: (s, a, b)))],
                lambda acc: acc, ("parallel", "parallel", "arbitrary"), name="rkv_proj")
            row = lambda name: P[name][j][None, :]
            w_mid = _proj(lerps, P["rwkv_w1"], j, a_lead=3, epilogue=jnp.tanh, out_dtype=BF16, name="lora_w1")
            w_pre = _proj(w_mid[None], P["rwkv_w2"], j, extras=[row("rwkv_w0")],
                          epilogue=lambda acc, b: acc + b, name="lora_w2")
            a_mid = _proj(lerps, P["rwkv_a1"], j, a_lead=4, out_dtype=BF16, name="lora_a1")
            a_gate = _proj(a_mid[None], P["rwkv_a2"], j, extras=[row("rwkv_a0")],
                           epilogue=lambda acc, b: jax.nn.sigmoid(acc + b), name="lora_a2")
            g_mid = _proj(lerps, P["rwkv_g1"], j, a_lead=5, epilogue=jax.nn.sigmoid, out_dtype=BF16,
                          name="lora_g1")
            g_out = _proj(g_mid[None], P["rwkv_g2"], j, name="lora_g2")
            if j == 0:
                vres = None
                v_first = rkv[2]
            else:
                v_mid = _proj(lerps, P["rwkv_v1"], j - 1, a_lead=2, out_dtype=BF16, name="lora_v1")
                v_gate = _proj(v_mid[None], P["rwkv_v2"], j - 1, extras=[P["rwkv_v0"][j - 1][None, :]],
                               epilogue=lambda acc, b: jax.nn.sigmoid(acc + b), name="lora_v2")
                vres = (v_gate, v_first)
            s0 = (jnp.zeros((n_seqs, n_rwkv_heads, n_state, n_state), F32) if wkv_in is None else wkv_in[j])
            vecs = (row("rwkv_k_k"), row("rwkv_k_a"), P["rwkv_r_k"][j].reshape(1, D),
                    row("rwkv_ln_w"), row("rwkv_ln_b"))
            if seq:
                shp = lambda t: t.reshape(n_seqs, T, D)
                z, s_bd = _rwkv_scan(rkv.reshape(3, n_seqs, T, D), shp(w_pre), shp(a_gate), shp(g_out),
                                     None if vres is None else tuple(shp(t) for t in vres),
                                     vecs, _to_block_diag(s0), SCAN_CHUNK, T)
                wkvs.append(_from_block_diag(s_bd, n_state))
                z = z.reshape(1, M, D)
            else:
                hs = lambda t: t.reshape(n_seqs, n_rwkv_heads, 1, n_state)
                seqs = [hs(rkv[0]), hs(rkv[1]), hs(rkv[2]), hs(w_pre), hs(a_gate), hs(g_out)]
                if vres is not None:
                    seqs += [hs(vres[0]), hs(vres[1])]
                z, s_new = _rwkv_step(seqs, [t.reshape(n_rwkv_heads, 1, n_state) for t in vecs], s0,
                                      STEP_MXU_ROUND)
                wkvs.append(s_new)
                z = z.reshape(1, M, D)
            x = _proj_residual(z, P["rwkv_w_o"], j, x.reshape(M, D), gate, rows_per_batch,
                               name="rwkv_out").reshape(Bx, R, D)
        else:
            h = _prep_plain(x, g_row, shift, scale)
            qkv = _qkv_proj(flat(h), P["attn_w_qkvf"], j,
                            jnp.stack([P["attn_q_g"][j], P["attn_k_g"][j]])[:, None, :], head_dim)
            w_f = jnp.pad(P["attn_w_qkvf"][j][:, 3 * D:], ((0, 0), (0, LANE - n_attn_heads)))[None]
            b_f = jnp.pad(P["attn_b_f"][j], (0, LANE - n_attn_heads))[None, :]
            logf = _proj(flat(h), w_f, 0, extras=[b_f], epilogue=lambda acc, b: -_softplus(-(acc + b)),
                         name="logf_proj")
            ks.append(qkv[1].reshape(n_seqs, T, n_attn_heads, head_dim))
            vs.append(qkv[2].reshape(n_seqs, T, n_attn_heads, head_dim))
            lfs.append(logf[:, :n_attn_heads].reshape(n_seqs, T, n_attn_heads))
            if not paged:
                f = _cumsum_rows(logf.reshape(n_seqs, T, LANE))[:, :, :n_attn_heads]
                f_bht = jnp.transpose(f, (0, 2, 1))
                o = _fox_prefill(qkv.reshape(3, n_seqs, T, D), f_bht[:, :, None, :], f_bht[:, :, :, None],
                                 head_dim)
            else:
                hv = lambda t: t.reshape(n_seqs, n_attn_heads, head_dim)
                o = _fox_decode(hv(qkv[0]), hv(qkv[1]), hv(qkv[2]),
                                logf[:, :n_attn_heads].reshape(n_seqs, n_attn_heads, 1),
                                P["cache_k"], P["cache_v"], P["cache_lft"], j, P["page_table"], head_dim)
            x = _proj_residual(o.reshape(1, M, D), P["attn_w_o"], j, x.reshape(M, D), gate, rows_per_batch,
                               name="attn_out").reshape(Bx, R, D)

        shift, scale, gate = mods[(i, 1)]
        g_row = P["norm_g"][i, 1][None, :]
        if i % 2 == 0:
            h = _prep_plain(x, g_row, shift, scale)
            hid = _ffn_up_dense(flat(h), P["ffn_w_gate"], P["ffn_w_up"], j)
            F = hid.shape[1]
            part = _proj_residual(hid[None], P["ffn_w_down"], j, None, None, rows_per_batch,
                                  k_blk=0, k_size=F // 2, final=False, name="ffn_down0")
            x = _proj_residual(hid[None], P["ffn_w_down"], j, x.reshape(M, D), gate, rows_per_batch,
                               k_blk=1, k_size=F // 2, prev=part, name="ffn_down1").reshape(Bx, R, D)
        else:
            w_router = jnp.pad(P["moe_w_router"][j], ((0, 0), (0, LANE - n_exp)))
            n_l = P["moe_w_gate"].shape[0]
            wg = P["moe_w_gate"].reshape((n_l * n_exp,) + P["moe_w_gate"].shape[2:])
            wu = P["moe_w_up"].reshape((n_l * n_exp,) + P["moe_w_up"].shape[2:])
            wd = P["moe_w_down"].reshape((n_l * n_exp,) + P["moe_w_down"].shape[2:])
            if seq:
                h32, gates, selmask = _prep_moe(x, g_row, shift, scale, w_router, n_exp, F32)
                tm = _div_tile(M, 512)
                row_src, tile_expert, row_gate, tok_pos = _route_metadata(
                    gates.reshape(M, LANE)[:, :n_exp], selmask.reshape(M, LANE)[:, :n_exp], n_exp, tm)
                a_sorted = _gather_rows(h32.reshape(M, 1, D), row_src, row_src.shape[0])
                hid = _moe_up_sorted(a_sorted[None], wg, wu, j * n_exp, tile_expert, tm)
                y = _moe_down_sorted(hid[None], wd, j * n_exp, tile_expert, row_gate, tm)
                x = _moe_combine(y.reshape(-1, 1, D), tok_pos, x, gate)
            else:
                hb, gates, _ = _prep_moe(x, g_row, shift, scale, w_router, n_exp, BF16)
                hid = _moe_up_all(hb, wg, wu, j * n_exp, n_exp)
                gates_t = jnp.transpose(gates.reshape(M, LANE)[:, :n_exp])[:, :, None]
                x = _moe_down_all(hid, wd, j * n_exp, gates_t, x, gate)
    return x, ks, vs, lfs, shifts, wkvs


def kernel(x_prompt, x_sample, c_prompt, c_sample, cache_k, cache_v, cache_logf, page_table, state_shift, state_wkv, ada_w, ada_b, norm_g, rwkv_mu, rwkv_w_rkv, rwkv_w0, rwkv_w1, rwkv_w2, rwkv_a0, rwkv_a1, rwkv_a2, rwkv_v0, rwkv_v1, rwkv_v2, rwkv_g1, rwkv_g2, rwkv_k_k, rwkv_k_a, rwkv_r_k, rwkv_ln_w, rwkv_ln_b, rwkv_w_o, attn_w_qkvf, attn_b_f, attn_q_g, attn_k_g, attn_w_o, ffn_w_gate, ffn_w_up, ffn_w_down, moe_w_router, moe_w_gate, moe_w_up, moe_w_down):
    B, T, D = x_prompt.shape
    Bd = x_sample.shape[0]
    depth = ada_w.shape[0]
    n_layers_attn, n_pool, page, n_attn_heads, head_dim = cache_k.shape

    P = dict(
        norm_g=norm_g, rwkv_mu=rwkv_mu, rwkv_w_rkv=rwkv_w_rkv, rwkv_w0=rwkv_w0, rwkv_w1=rwkv_w1,
        rwkv_w2=rwkv_w2, rwkv_a0=rwkv_a0, rwkv_a1=rwkv_a1, rwkv_a2=rwkv_a2, rwkv_v0=rwkv_v0,
        rwkv_v1=rwkv_v1, rwkv_v2=rwkv_v2, rwkv_g1=rwkv_g1, rwkv_g2=rwkv_g2, rwkv_k_k=rwkv_k_k,
        rwkv_k_a=rwkv_k_a, rwkv_r_k=rwkv_r_k, rwkv_ln_w=rwkv_ln_w, rwkv_ln_b=rwkv_ln_b,
        rwkv_w_o=rwkv_w_o, attn_w_qkvf=attn_w_qkvf, attn_b_f=attn_b_f, attn_q_g=attn_q_g,
        attn_k_g=attn_k_g, attn_w_o=attn_w_o, ffn_w_gate=ffn_w_gate, ffn_w_up=ffn_w_up,
        ffn_w_down=ffn_w_down, moe_w_router=moe_w_router, moe_w_gate=moe_w_gate, moe_w_up=moe_w_up,
        moe_w_down=moe_w_down,
        state_dims=(state_wkv.shape[2], state_wkv.shape[3]),
        cache_k=cache_k.reshape(n_layers_attn, n_pool, page, n_attn_heads * head_dim),
        cache_v=cache_v.reshape(n_layers_attn, n_pool, page, n_attn_heads * head_dim),
        cache_lft=jnp.swapaxes(cache_logf, 2, 3),
        page_table=page_table,
    )

    n_rows = B + Bd
    rows_pad = -(-n_rows // SUBLANE) * SUBLANE
    c_all = jnp.pad(jnp.concatenate([c_prompt, c_sample], axis=0), ((0, rows_pad - n_rows), (0, 0)))
    mods_all = _ada_mods(c_all, ada_w.reshape(depth * 2, D, 3 * D), ada_b.reshape(depth * 2, 1, 3 * D))

    def split_mods(r0, r1, per_row):
        out = {}
        for i in range(depth):
            for s in range(2):
                m = mods_all[i * 2 + s, r0:r1]
                parts = [m[:, k * D:(k + 1) * D] for k in range(3)]
                out[(i, s)] = tuple(p[None] if per_row else p[:, None, :] for p in parts)
        return out

    yp, ks, vs, lfs, shifts, wkvs = _trunk(
        x_prompt, split_mods(0, B, False), T, True, P, None, None, False)
    ys, ks2, vs2, lfs2, shifts2, wkvs2 = _trunk(
        x_sample.reshape(1, Bd, D), split_mods(B, B + Bd, True), 1, False, P, state_shift, state_wkv, True)
    return (yp, ys.reshape(Bd, 1, D),
            jnp.stack(ks), jnp.stack(vs), jnp.stack(lfs),
            jnp.stack(ks2), jnp.stack(vs2), jnp.stack(lfs2),
            jnp.stack(shifts), jnp.stack(wkvs), jnp.stack(shifts2), jnp.stack(wkvs2))
```

```python
import functools
import math

import jax
import jax.numpy as jnp
from jax import lax
from jax.experimental import pallas as pl
from jax.experimental.pallas import tpu as pltpu

F32 = jnp.float32
BF16 = jnp.bfloat16

NORM_EPS = 1e-6
GN_EPS = 64e-5
NEG_INF = -1e30
TOP_K = 2
N_LERP = 6

LANE = 128
SUBLANE = 8
VMEM_BYTES = 64 * 1024 * 1024
VMEM_CAP = VMEM_BYTES - 6 * 1024 * 1024
MIB = 1024 * 1024

HI = lax.Precision.HIGHEST

SCAN_CHUNK = 64
SCAN_PAIRS = 8
SCAN_ROWS = 256
STEP_MXU_ROUND = True


def _params(sem, est_bytes):
    limit = min(max(int(est_bytes) + 4 * MIB, 32 * MIB), VMEM_CAP)
    return pltpu.CompilerParams(dimension_semantics=sem, vmem_limit_bytes=limit)


def _nbytes(shape, dtype):
    return math.prod(shape) * jnp.dtype(dtype).itemsize


def _div_tile(n, pref):
    if n <= pref:
        return n
    t = pref
    while n % t:
        t //= 2
    return t


def _dot(a, b):
    return jnp.dot(a.astype(BF16), b.astype(BF16), preferred_element_type=F32)


def _dot_nt(a, b):
    return lax.dot_general(a.astype(BF16), b.astype(BF16), (((1,), (1,)), ((), ())),
                           preferred_element_type=F32)


def _dot_tn(a, b):
    return lax.dot_general(a.astype(BF16), b.astype(BF16), (((0,), (0,)), ((), ())),
                           preferred_element_type=F32)


def _softplus(z):
    return jnp.maximum(z, 0.0) + jnp.log1p(jnp.exp(-jnp.abs(z)))


def _rms_mod(x, g, shift, scale):
    y = x * lax.rsqrt(jnp.mean(x * x, axis=-1, keepdims=True) + NORM_EPS)
    return (y * g) * (1.0 + scale) + shift


def _ada_body(c_ref, w_ref, b_ref, o_ref):
    c = c_ref[...]
    a = c * jax.nn.sigmoid(c)
    o_ref[0] = _dot(a, w_ref[0]) + b_ref[0]


def _ada_mods(c_all, ada_w, ada_b):
    S, D, N = ada_w.shape
    Rp = c_all.shape[0]
    tn = _div_tile(N, 512)
    est = 2 * (_nbytes((D, tn), F32) + _nbytes((Rp, D), F32) + 2 * _nbytes((Rp, tn), F32))
    return pl.pallas_call(
        _ada_body,
        grid=(S, N // tn),
        in_specs=[pl.BlockSpec((Rp, D), lambda s, j: (0, 0)),
                  pl.BlockSpec((1, D, tn), lambda s, j: (s, 0, j)),
                  pl.BlockSpec((1, 1, tn), lambda s, j: (s, 0, j))],
        out_specs=pl.BlockSpec((1, Rp, tn), lambda s, j: (s, 0, j)),
        out_shape=jax.ShapeDtypeStruct((S, Rp, N), F32),
        compiler_params=_params(("parallel", "parallel"), est),
        name="ada_mods",
    )(c_all, ada_w, ada_b)


def _mod_spec(mod, R, tr):
    D = mod.shape[-1]
    if mod.shape[1] == 1:
        return pl.BlockSpec((1, 1, D), lambda b, r: (b, 0, 0))
    assert mod.shape[1] == R
    return pl.BlockSpec((1, tr, D), lambda b, r: (b, r, 0))


def _prep_plain_body(x_ref, g_ref, sh_ref, sc_ref, h_ref):
    h_ref[0] = _rms_mod(x_ref[0], g_ref[...], sh_ref[0], sc_ref[0]).astype(h_ref.dtype)


def _prep_plain(x, g, shift, scale):
    Bx, R, D = x.shape
    tr = _div_tile(R, 256)
    est = 2 * (_nbytes((tr, D), F32) * 3 + _nbytes((tr, D), BF16))
    return pl.pallas_call(
        _prep_plain_body,
        grid=(Bx, R // tr),
        in_specs=[pl.BlockSpec((1, tr, D), lambda b, r: (b, r, 0)),
                  pl.BlockSpec((1, D), lambda b, r: (0, 0)),
                  _mod_spec(shift, R, tr), _mod_spec(scale, R, tr)],
        out_specs=pl.BlockSpec((1, tr, D), lambda b, r: (b, r, 0)),
        out_shape=jax.ShapeDtypeStruct((Bx, R, D), BF16),
        compiler_params=_params(("parallel", "parallel"), est),
        name="prep_plain",
    )(x, g, shift, scale)


def _prep_moe_body(n_exp, x_ref, g_ref, sh_ref, sc_ref, wr_ref, h_ref, gates_ref, sel_ref):
    h = _rms_mod(x_ref[0], g_ref[...], sh_ref[0], sc_ref[0])
    h_ref[0] = h.astype(h_ref.dtype)
    logits = _dot(h, wr_ref[...])
    lane = lax.broadcasted_iota(jnp.int32, logits.shape, 1)
    logits = jnp.where(lane < n_exp, logits, NEG_INF)
    e = jnp.exp(logits - jnp.max(logits, axis=-1, keepdims=True))
    p = e / jnp.sum(e, axis=-1, keepdims=True)
    p = jnp.where(lane < n_exp, p, -1.0)
    lane_f = lane.astype(F32)
    m1 = jnp.max(p, axis=-1, keepdims=True)
    i1 = jnp.min(jnp.where(p == m1, lane_f, float(LANE)), axis=-1, keepdims=True)
    p2 = jnp.where(lane_f == i1, -1.0, p)
    m2 = jnp.max(p2, axis=-1, keepdims=True)
    i2 = jnp.min(jnp.where(p2 == m2, lane_f, float(LANE)), axis=-1, keepdims=True)
    tot = m1 + m2
    gates_ref[0] = jnp.where(lane_f == i1, m1 / tot, jnp.where(lane_f == i2, m2 / tot, 0.0))
    sel_ref[0] = jnp.where((lane_f == i1) | (lane_f == i2), 1.0, 0.0)


def _prep_moe(x, g, shift, scale, w_router_pad, n_exp, h_dtype):
    Bx, R, D = x.shape
    tr = _div_tile(R, 256)
    est = 2 * (_nbytes((tr, D), F32) * 4 + _nbytes((D, LANE), F32))
    return pl.pallas_call(
        functools.partial(_prep_moe_body, n_exp),
        grid=(Bx, R // tr),
        in_specs=[pl.BlockSpec((1, tr, D), lambda b, r: (b, r, 0)),
                  pl.BlockSpec((1, D), lambda b, r: (0, 0)),
                  _mod_spec(shift, R, tr), _mod_spec(scale, R, tr),
                  pl.BlockSpec((D, LANE), lambda b, r: (0, 0))],
        out_specs=[pl.BlockSpec((1, tr, D), lambda b, r: (b, r, 0)),
                   pl.BlockSpec((1, tr, LANE), lambda b, r: (b, r, 0)),
                   pl.BlockSpec((1, tr, LANE), lambda b, r: (b, r, 0))],
        out_shape=[jax.ShapeDtypeStruct((Bx, R, D), h_dtype),
                   jax.ShapeDtypeStruct((Bx, R, LANE), F32),
                   jax.ShapeDtypeStruct((Bx, R, LANE), F32)],
        compiler_params=_params(("parallel", "parallel"), est),
        name="prep_moe",
    )(x, g, shift, scale, w_router_pad)


def _prep_rwkv_body(seq, x_ref, g_ref, sh_ref, sc_ref, mu_ref, sp_ref, lerp_ref, hl_ref, carry_ref):
    h = _rms_mod(x_ref[0], g_ref[...], sh_ref[0], sc_ref[0])
    tr = h.shape[0]
    if seq:
        @pl.when(pl.program_id(1) == 0)
        def _():
            carry_ref[...] = sp_ref[0]
        row = lax.broadcasted_iota(jnp.int32, h.shape, 0)
        h_prev = jnp.where(row == 0, carry_ref[...], pltpu.roll(h, 1, 0))
        carry_ref[...] = h[tr - 1:tr, :]
        hl_ref[0] = h[tr - 1:tr, :]
    else:
        h_prev = sp_ref[0]
        hl_ref[0] = h
    xx = h_prev - h
    for i in range(N_LERP):
        lerp_ref[i, 0] = (h + xx * mu_ref[i:i + 1, :]).astype(lerp_ref.dtype)


def _prep_rwkv(x, g, shift, scale, mu, shift_prev, seq):
    Bx, R, D = x.shape
    tr = _div_tile(R, 128)
    est = 2 * (_nbytes((tr, D), F32) * 4 + _nbytes((N_LERP, tr, D), BF16)) + _nbytes((N_LERP, D), F32)
    if seq:
        sp_spec = pl.BlockSpec((1, 1, D), lambda b, r: (b, 0, 0))
        hl_spec = pl.BlockSpec((1, 1, D), lambda b, r: (b, 0, 0))
        hl_shape = jax.ShapeDtypeStruct((Bx, 1, D), F32)
    else:
        sp_spec = pl.BlockSpec((1, tr, D), lambda b, r: (b, r, 0))
        hl_spec = pl.BlockSpec((1, tr, D), lambda b, r: (b, r, 0))
        hl_shape = jax.ShapeDtypeStruct((Bx, R, D), F32)
    return pl.pallas_call(
        functools.partial(_prep_rwkv_body, seq),
        grid=(Bx, R // tr),
        in_specs=[pl.BlockSpec((1, tr, D), lambda b, r: (b, r, 0)),
                  pl.BlockSpec((1, D), lambda b, r: (0, 0)),
                  _mod_spec(shift, R, tr), _mod_spec(scale, R, tr),
                  pl.BlockSpec((N_LERP, D), lambda b, r: (0, 0)),
                  sp_spec],
        out_specs=[pl.BlockSpec((N_LERP, 1, tr, D), lambda b, r: (0, b, r, 0)), hl_spec],
        out_shape=[jax.ShapeDtypeStruct((N_LERP, Bx, R, D), BF16), hl_shape],
        scratch_shapes=[pltpu.VMEM((1, D), F32)],
        compiler_params=_params(("parallel", "arbitrary"), est),
        name="prep_rwkv",
    )(x, g, shift, scale, mu, shift_prev)


def _linear_body(n_w, n_ex, n_pf, n_into, epilogue, is_padding, *refs):
    pf_refs, refs = refs[:n_pf], refs[n_pf:]
    a_ref = refs[0]
    w_refs = refs[1:1 + n_w]
    ex_refs = refs[1 + n_w:1 + n_w + n_ex]
    out_refs = refs[1 + n_w + n_ex + n_into:]

    def compute():
        a = a_ref[...].reshape(a_ref.shape[-2:])
        accs = [_dot(a, w[...].reshape(w.shape[-2:])) for w in w_refs]
        exs = [e[...].reshape(e.shape[-2:]) for e in ex_refs]
        res = epilogue(*accs, *exs)
        if not isinstance(res, (tuple, list)):
            res = (res,)
        for o, v in zip(out_refs, res):
            o[...] = v.astype(o.dtype).reshape(o.shape)

    if is_padding is None:
        compute()
    else:
        pad = is_padding(*pf_refs)
        pl.when(jnp.logical_not(pad))(compute)

        @pl.when(pad)
        def _():
            for o in out_refs:
                o[...] = jnp.zeros(o.shape, o.dtype)


def _linear(grid, a, a_spec, ws, w_spec, extras, outs, epilogue, sem, prefetch=None, into=None,
            is_padding=None, name="linear"):
    def blk_bytes(spec, dtype):
        dims = [d.block_size if isinstance(d, pl.Element) else (1 if d is None else d)
                for d in spec.block_shape]
        return _nbytes(dims, dtype)

    est = 2 * blk_bytes(a_spec, a.dtype)
    est += sum(2 * blk_bytes(w_spec, w.dtype) + blk_bytes(w_spec, BF16) for w in ws)
    est += sum(2 * blk_bytes(s, e.dtype) for e, s in extras)
    est += sum(4 * blk_bytes(s, F32) for o, s in outs)
    n_pf = 0 if prefetch is None else 1
    body = functools.partial(_linear_body, len(ws), len(extras), n_pf, 0 if into is None else 1, epilogue,
                             is_padding)
    in_specs = [a_spec] + [w_spec] * len(ws) + [s for _, s in extras]
    out_specs = [s for _, s in outs]
    out_shape = [o for o, _ in outs]
    args = [a] + list(ws) + [e for e, _ in extras]
    aliases = {}
    if into is not None:
        aliases = {n_pf + len(args): 0}
        in_specs.append(pl.BlockSpec(memory_space=pl.ANY))
        args.append(into)
    if prefetch is None:
        call = pl.pallas_call(body, grid=grid, in_specs=in_specs, out_specs=out_specs,
                              out_shape=out_shape, input_output_aliases=aliases,
                              compiler_params=_params(sem, est), name=name)
        res = call(*args)
    else:
        gs = pltpu.PrefetchScalarGridSpec(num_scalar_prefetch=1, grid=grid, in_specs=in_specs,
                                          out_specs=out_specs)
        call = pl.pallas_call(body, grid_spec=gs, out_shape=out_shape, input_output_aliases=aliases,
                              compiler_params=_params(sem, est), name=name)
        res = call(prefetch, *args)
    return res


def _row_tile(M):
    return _div_tile(M, 1024)


def _gate_spec_ij(gate, rows_per_batch, tm, tn):
    if gate.shape[1] == 1:
        return pl.BlockSpec((1, 1, tn), lambda i, j: ((i * tm) // rows_per_batch, 0, j))
    assert gate.shape[0] == 1 and gate.shape[1] == tm
    return pl.BlockSpec((1, tm, tn), lambda i, j: (0, 0, j))


def _proj(a, w, w_lead, *, a_lead=0, n_cols=None, epilogue=None, extras=(), out_dtype=F32, tn_pref=512,
          name="proj"):
    _, M, K = a.shape
    N = w.shape[2] if n_cols is None else n_cols
    tm = _row_tile(M)
    tn = _div_tile(N, tn_pref if M > 64 else 1024)
    ex = []
    for e in extras:
        if e.shape[0] == 1:
            ex.append((e, pl.BlockSpec((1, tn), lambda i, j: (0, j))))
        else:
            ex.append((e, pl.BlockSpec((tm, tn), lambda i, j: (i, j))))
    epi = epilogue if epilogue is not None else (lambda acc: acc)
    (out,) = _linear(
        (M // tm, N // tn), a,
        pl.BlockSpec((1, tm, K), lambda i, j: (a_lead, i, 0)),
        [w], pl.BlockSpec((1, K, tn), lambda i, j: (w_lead, 0, j)),
        ex,
        [(jax.ShapeDtypeStruct((M, N), out_dtype), pl.BlockSpec((tm, tn), lambda i, j: (i, j)))],
        epi, ("parallel", "arbitrary"), name=name)
    return out


def _proj_residual(a, w, w_lead, x, gate, rows_per_batch, *, k_blk=0, k_size=None, prev=None,
                   final=True, name="proj_res"):
    M = a.shape[1]
    K = a.shape[2] if k_size is None else k_size
    N = w.shape[2]
    tm = _row_tile(min(M, rows_per_batch) if rows_per_batch > 1 else M)
    tn = _div_tile(N, 256 if M > 64 else 512)
    tile = lambda arr: (arr, pl.BlockSpec((tm, tn), lambda i, j: (i, j)))
    ex = []
    if prev is not None:
        ex.append(tile(prev))
    if final:
        ex.append(tile(x))
        ex.append((gate, _gate_spec_ij(gate, rows_per_batch, tm, tn)))

    def epi(acc, *e):
        e = list(e)
        if prev is not None:
            acc = acc + e.pop(0)
        if final:
            xv, gv = e
            acc = xv + gv * acc
        return acc

    (out,) = _linear(
        (M // tm, N // tn), a,
        pl.BlockSpec((1, tm, K), lambda i, j: (0, i, k_blk)),
        [w], pl.BlockSpec((1, K, tn), lambda i, j: (w_lead, k_blk, j)),
        ex,
        [(jax.ShapeDtypeStruct((M, N), F32), pl.BlockSpec((tm, tn), lambda i, j: (i, j)))],
        epi, ("parallel", "arbitrary"), name=name)
    return out


def _scan_body(C, n_par, n_chunks, has_vres, *refs):
    if has_vres:
        (r_ref, k_ref, v_ref, w_ref, a_ref, g_ref, vg_ref, vf_ref,
         kk_ref, ka_ref, rk_ref, lnw_ref, lnb_ref, s0_ref, z_ref, st_ref, s_scr) = refs
    else:
        (r_ref, k_ref, v_ref, w_ref, a_ref, g_ref,
         kk_ref, ka_ref, rk_ref, lnw_ref, lnb_ref, s0_ref, z_ref, st_ref, s_scr) = refs
    t_id = pl.program_id(2)

    @pl.when(t_id == 0)
    def _():
        s_scr[...] = s0_ref[0]

    half = LANE // 2
    lane = lax.broadcasted_iota(jnp.int32, (1, LANE), 1)
    m0 = lane < half
    r2 = lax.broadcasted_iota(jnp.int32, (2 * C, 2 * C), 0) % C
    c2 = lax.broadcasted_iota(jnp.int32, (2 * C, 2 * C), 1) % C
    stril = r2 > c2
    tril = r2 >= c2
    tri_c = (lax.broadcasted_iota(jnp.int32, (C, C), 0)
             >= lax.broadcasted_iota(jnp.int32, (C, C), 1)).astype(F32)
    n_levels = max(1, int(math.log2(C)))
    inv_half = 1.0 / half

    def seg_sum(x):
        s0 = jnp.sum(jnp.where(m0, x, 0.0), axis=-1, keepdims=True)
        s1 = jnp.sum(jnp.where(m0, 0.0, x), axis=-1, keepdims=True)
        return jnp.where(m0, s0, s1)

    def stack(x):
        return jnp.concatenate([jnp.where(m0, x, 0.0), jnp.where(m0, 0.0, x)], axis=0)

    def each(f, *lists):
        return [f(*xs) for xs in zip(*lists)]

    lss = [slice(i * LANE, (i + 1) * LANE) for i in range(n_par)]

    def chunk(c, states):
        sl = pl.ds(pl.multiple_of(c * C, C), C)
        S = list(states)
        r = [r_ref[0, 0, sl, ls] for ls in lss]
        k = [k_ref[0, 0, sl, ls] for ls in lss]
        v = [v_ref[0, 0, sl, ls] for ls in lss]
        a = [a_ref[0, sl, ls] for ls in lss]
        logd = [-jnp.exp(-_softplus(-w_ref[0, sl, ls]) - 0.5) for ls in lss]
        if has_vres:
            v = [vi + (vf_ref[0, sl, ls] - vi) * vg_ref[0, sl, ls] for vi, ls in zip(v, lss)]
        cum = each(lambda x: jnp.dot(tri_c, x, precision=HI, preferred_element_type=F32), logd)
        kkr = [ki * kk_ref[:, ls] for ki, ls in zip(k, lss)]
        kk = each(lambda x: x / jnp.maximum(jnp.sqrt(seg_sum(x * x)), 1e-12), kkr)
        k2 = [ki * (1.0 + (ai - 1.0) * ka_ref[:, ls]) for ki, ai, ls in zip(k, a, lss)]
        eg = each(jnp.exp, cum)
        eng = each(lambda x: jnp.exp(-x), cum)
        a_st = each(lambda kki, ci, li: stack(-kki * jnp.exp(ci - li)), kk, cum, logd)
        r_st = each(lambda ri, e: stack(ri * e), r, eg)
        b_st = each(lambda kki, ai, e: stack(kki * ai * e), kk, a, eng)
        k_st = each(lambda ki, e: stack(ki * e), k2, eng)
        v_st = each(stack, v)
        ar = each(lambda x, y: jnp.concatenate([x, y], axis=0), a_st, r_st)
        xb = each(_dot_nt, ar, b_st)
        xk = each(_dot_nt, ar, k_st)
        p = each(lambda x: jnp.where(stril, x[:2 * C], 0.0), xb)
        lak = each(lambda x: jnp.where(stril, x[:2 * C], 0.0), xk)
        mrb = each(lambda x: jnp.where(tril, x[2 * C:], 0.0), xb)
        mrk = each(lambda x: jnp.where(tril, x[2 * C:], 0.0), xk)
        q = p
        for _ in range(1, n_levels):
            p = each(_dot, p, p)
            q = each(lambda qi, pi: qi + pi + _dot(qi, pi), q, p)
        lv = each(_dot, lak, v_st)
        mv = each(_dot, mrk, v_st)
        a_s = each(_dot_nt, ar, S)
        wm = each(lambda x, y: x[:2 * C] + y, a_s, lv)
        u = each(lambda qi, w: w + _dot(qi, w), q, wm)
        y_st = each(lambda x, m, ui, mvi: x[2 * C:] + _dot(m, ui) + mvi, a_s, mrb, u, mv)
        y = each(lambda x: x[:C] + x[C:], y_st)
        eg_last = each(lambda e: e[C - 1:C, :], eg)
        S_new = each(lambda Si, e, ui, b, vs, ks: Si * e + _dot_tn(ui, b * e) + _dot_tn(vs, ks * e),
                     S, eg_last, u, b_st, v_st, k_st)
        mean = each(lambda x: seg_sum(x) * inv_half, y)
        yc = each(lambda x, m: x - m, y, mean)
        var = each(lambda x: seg_sum(x * x) * inv_half, yc)
        for i, ls in enumerate(lss):
            yn = yc[i] * lax.rsqrt(var[i] + GN_EPS) * lnw_ref[:, ls] + lnb_ref[:, ls]
            bonus = seg_sum(r[i] * k2[i] * rk_ref[:, ls]) * v[i]
            z_ref[0, sl, ls] = ((yn + bonus) * g_ref[0, sl, ls]).astype(z_ref.dtype)
        return tuple(S_new)

    states = lax.fori_loop(0, n_chunks, chunk, tuple(s_scr[i] for i in range(n_par)))
    for i in range(n_par):
        s_scr[i] = states[i]

    @pl.when(t_id == pl.num_programs(2) - 1)
    def _():
        st_ref[0] = s_scr[...]


def _rwkv_scan(rkv, w_pre, a_gate, g, vres, vecs, s0_bd, C):
    _, B, T, D = rkv.shape
    n_par = SCAN_PAIRS if (D // LANE) % SCAN_PAIRS == 0 else 1
    W = n_par * LANE
    nb = D // W
    tb = _div_tile(T, SCAN_ROWS)
    assert tb % C == 0
    seq_spec = pl.BlockSpec((1, tb, W), lambda b, h, t: (b, t, h))
    rkv_spec = lambda i: pl.BlockSpec((1, 1, tb, W), lambda b, h, t: (i, b, t, h))
    vec_spec = pl.BlockSpec((1, W), lambda b, h, t: (0, h))
    st_spec = pl.BlockSpec((1, n_par, LANE, LANE), lambda b, h, t: (b, h, 0, 0))
    has_vres = vres is not None
    n_seq = 6 + (2 if has_vres else 0)
    in_specs = [rkv_spec(0), rkv_spec(1), rkv_spec(2)] + [seq_spec] * (n_seq - 3) + [vec_spec] * 5 + [st_spec]
    args = [rkv, rkv, rkv, w_pre, a_gate, g] + (list(vres) if has_vres else []) + list(vecs) + [s0_bd]
    est = 2 * (n_seq * _nbytes((tb, W), F32) + _nbytes((tb, W), BF16) + 2 * _nbytes((n_par, LANE, LANE), F32))
    est += n_par * 64 * _nbytes((2 * C, LANE), F32)
    return pl.pallas_call(
        functools.partial(_scan_body, C, n_par, tb // C, has_vres),
        grid=(B, nb, T // tb),
        in_specs=in_specs,
        out_specs=[seq_spec, st_spec],
        out_shape=[jax.ShapeDtypeStruct((B, T, D), BF16),
                   jax.ShapeDtypeStruct((B, D // LANE, LANE, LANE), F32)],
        scratch_shapes=[pltpu.VMEM((n_par, LANE, LANE), F32)],
        compiler_params=_params(("parallel", "parallel", "arbitrary"), est),
        name="rwkv_scan",
    )(*args)


def _to_block_diag(s):
    B, H, N, _ = s.shape
    s = s.reshape(B, H // 2, 2, N, N)
    z = jnp.zeros_like(s[:, :, 0])
    top = jnp.concatenate([s[:, :, 0], z], axis=-1)
    bot = jnp.concatenate([z, s[:, :, 1]], axis=-1)
    return jnp.concatenate([top, bot], axis=-2)


def _from_block_diag(s_bd, N):
    B, nb = s_bd.shape[:2]
    return jnp.stack([s_bd[:, :, :N, :N], s_bd[:, :, N:, N:]], axis=2).reshape(B, 2 * nb, N, N)


def _rwkv_step_body(has_vres, mxu_round, *refs):
    if has_vres:
        (r_ref, k_ref, v_ref, w_ref, a_ref, g_ref, vg_ref, vf_ref,
         kk_ref, ka_ref, rk_ref, lnw_ref, lnb_ref, s0_ref, z_ref, st_ref) = refs
    else:
        (r_ref, k_ref, v_ref, w_ref, a_ref, g_ref,
         kk_ref, ka_ref, rk_ref, lnw_ref, lnb_ref, s0_ref, z_ref, st_ref) = refs
    r, k, v, a = r_ref[0], k_ref[0], v_ref[0], a_ref[0]
    S = s0_ref[0]
    N = S.shape[-1]
    eye = (lax.broadcasted_iota(jnp.int32, (N, N), 0) == lax.broadcasted_iota(jnp.int32, (N, N), 1))
    col = lambda x: jnp.sum(jnp.where(eye, x, 0.0), axis=-1, keepdims=True)
    row = lambda x: jnp.sum(jnp.where(eye, x, 0.0), axis=-2, keepdims=True)
    rnd = (lambda x: x.astype(BF16).astype(F32)) if mxu_round else (lambda x: x)
    decay = jnp.exp(-jnp.exp(-_softplus(-w_ref[0]) - 0.5))
    if has_vres:
        v = v + (vf_ref[0] - v) * vg_ref[0]
    kkr = k * kk_ref[...]
    kk = kkr / jnp.maximum(jnp.sqrt(jnp.sum(kkr * kkr, axis=-1, keepdims=True)), 1e-12)
    k2 = k * (1.0 + (a - 1.0) * ka_ref[...])
    sa = jnp.sum(rnd(S) * rnd(-kk), axis=-1, keepdims=True)
    S_new = S * decay + sa * (kk * a) + col(v) * k2
    st_ref[0] = S_new
    y = jnp.sum(rnd(S_new) * rnd(r), axis=-1, keepdims=True)
    mean = jnp.mean(y, axis=-2, keepdims=True)
    yc = y - mean
    var = jnp.mean(yc * yc, axis=-2, keepdims=True)
    yn = row(yc * lax.rsqrt(var + GN_EPS)) * lnw_ref[...] + lnb_ref[...]
    bonus = jnp.sum(r * k2 * rk_ref[...], axis=-1, keepdims=True) * v
    z_ref[0] = (yn + bonus) * g_ref[0]


def _rwkv_step(seqs, vecs, s0, mxu_round):
    Bd, H, N, _ = s0.shape
    row_spec = pl.BlockSpec((1, H, 1, N), lambda b: (b, 0, 0, 0))
    vec_spec = pl.BlockSpec((H, 1, N), lambda b: (0, 0, 0))
    st_spec = pl.BlockSpec((1, H, N, N), lambda b: (b, 0, 0, 0))
    est = 24 * _nbytes((H, N, LANE), F32)
    return pl.pallas_call(
        functools.partial(_rwkv_step_body, len(seqs) == 8, mxu_round),
        grid=(Bd,),
        in_specs=[row_spec] * len(seqs) + [vec_spec] * 5 + [st_spec],
        out_specs=[row_spec, st_spec],
        out_shape=[jax.ShapeDtypeStruct((Bd, H, 1, N), F32), jax.ShapeDtypeStruct((Bd, H, N, N), F32)],
        compiler_params=_params(("parallel",), est),
        name="rwkv_step",
    )(*seqs, *vecs, s0)


def _cumsum_body(tb, lf_ref, f_ref):
    T = lf_ref.shape[1]
    tri = (lax.broadcasted_iota(jnp.int32, (tb, tb), 0)
           >= lax.broadcasted_iota(jnp.int32, (tb, tb), 1)).astype(F32)

    def blk(i, carry):
        sl = pl.ds(pl.multiple_of(i * tb, tb), tb)
        f = jnp.dot(tri, lf_ref[0, sl, :], precision=HI, preferred_element_type=F32) + carry
        f_ref[0, sl, :] = f
        return f[tb - 1:tb, :]

    lax.fori_loop(0, T // tb, blk, jnp.zeros((1, lf_ref.shape[2]), F32))


def _cumsum_rows(lf):
    B, T, W = lf.shape
    tb = _div_tile(T, 256)
    return pl.pallas_call(
        functools.partial(_cumsum_body, tb),
        grid=(B,),
        in_specs=[pl.BlockSpec((1, T, W), lambda b: (b, 0, 0))],
        out_specs=pl.BlockSpec((1, T, W), lambda b: (b, 0, 0)),
        out_shape=jax.ShapeDtypeStruct((B, T, W), F32),
        compiler_params=_params(("parallel",), 4 * _nbytes((T, W), F32)),
        name="logf_cumsum",
    )(lf)


def _fox_prefill_body(tq, scale, q_ref, k_ref, v_ref, fr_ref, fc_ref, o_ref, kb_ref, vb_ref):
    T = q_ref.shape[2]
    kb_ref[...] = k_ref[0, 0].astype(BF16)
    vb_ref[...] = v_ref[0, 0].astype(BF16)
    causal = (lax.broadcasted_iota(jnp.int32, (tq, tq), 1)
              <= lax.broadcasted_iota(jnp.int32, (tq, tq), 0))

    def q_blk(qi, _):
        qs = pl.ds(pl.multiple_of(qi * tq, tq), tq)
        q = q_ref[0, 0, qs, :].astype(BF16)
        fq = fc_ref[0, 0, qs, :]

        def kv_blk(ks, carry, diagonal):
            m, l, acc = carry
            s = _dot_nt(q, kb_ref[ks, :]) * scale + fq - fr_ref[0, 0, :, ks]
            if diagonal:
                s = jnp.where(causal, s, NEG_INF)
            m_new = jnp.maximum(m, jnp.max(s, axis=-1, keepdims=True))
            alpha = jnp.exp(m - m_new)
            p = jnp.exp(s - m_new)
            l = alpha * l + jnp.sum(p, axis=-1, keepdims=True)
            acc = alpha * acc + _dot(p, vb_ref[ks, :])
            return m_new, l, acc

        init = (jnp.full((tq, 1), NEG_INF, F32), jnp.zeros((tq, 1), F32), jnp.zeros((tq, LANE), F32))
        carry = lax.fori_loop(
            0, qi, lambda kj, c: kv_blk(pl.ds(pl.multiple_of(kj * tq, tq), tq), c, False), init)
        _, l, acc = kv_blk(qs, carry, True)
        o_ref[0, qs, :] = (acc / l).astype(o_ref.dtype)
        return 0

    lax.fori_loop(0, T // tq, q_blk, 0)


def _fox_prefill(qk, v, f_row, f_col, head_dim):
    _, B, T, D = qk.shape
    H = D // head_dim
    assert head_dim == LANE
    tq = _div_tile(T, 512)
    spec = lambda i: pl.BlockSpec((1, 1, T, LANE), lambda b, h: (i, b, 0, h))
    est = 2 * (3 * _nbytes((T, LANE), F32) + _nbytes((T, LANE), F32) + 2 * _nbytes((T, LANE), BF16))
    est += 6 * _nbytes((tq, tq), F32)
    return pl.pallas_call(
        functools.partial(_fox_prefill_body, tq, head_dim ** -0.5),
        grid=(B, H),
        in_specs=[spec(0), spec(1), spec(0),
                  pl.BlockSpec((1, 1, 1, T), lambda b, h: (b, h, 0, 0)),
                  pl.BlockSpec((1, 1, T, 1), lambda b, h: (b, h, 0, 0))],
        out_specs=pl.BlockSpec((1, T, LANE), lambda b, h: (b, 0, h)),
        out_shape=jax.ShapeDtypeStruct((B, T, D), BF16),
        scratch_shapes=[pltpu.VMEM((T, LANE), BF16), pltpu.VMEM((T, LANE), BF16)],
        compiler_params=_params(("parallel", "parallel"), est),
        name="fox_prefill",
    )(qk, qk, v, f_row, f_col)


def _fox_decode_body(n_heads, scale, pt_ref, q_ref, kn_ref, vn_ref, lfn_ref, kp_ref, vp_ref, lfp_ref,
                     o_ref, qm_ref, m_ref, l_ref, acc_ref, carry_ref):
    p_id = pl.program_id(1)
    n_pages = pl.num_programs(1)
    P = kp_ref.shape[0]
    n_grp = n_heads // SUBLANE
    sub = lax.broadcasted_iota(jnp.int32, (SUBLANE, LANE), 0)

    @pl.when(p_id == 0)
    def _():
        q = q_ref[0]
        for h in range(n_heads):
            qm_ref[h] = jnp.where(sub == h % SUBLANE, q[h:h + 1, :], 0.0)
        m_ref[...] = jnp.full(m_ref.shape, NEG_INF, F32)
        l_ref[...] = jnp.zeros(l_ref.shape, F32)
        acc_ref[...] = jnp.zeros(acc_ref.shape, F32)
        carry_ref[...] = lfn_ref[0]

    lf = lfp_ref[...]
    later = (lax.broadcasted_iota(jnp.int32, (P, P), 0)
             > lax.broadcasted_iota(jnp.int32, (P, P), 1)).astype(F32)
    bias = jnp.dot(lf, later, precision=HI, preferred_element_type=F32) + carry_ref[...]
    carry_ref[...] = carry_ref[...] + jnp.sum(lf, axis=-1, keepdims=True)

    s_grp = []
    for gi in range(n_grp):
        s = jnp.zeros((SUBLANE, P), F32)
        for hh in range(SUBLANE):
            h = gi * SUBLANE + hh
            s = s + _dot_nt(qm_ref[h], kp_ref[:, h * LANE:(h + 1) * LANE])
        s_grp.append(s)
    s = jnp.concatenate(s_grp, axis=0) * scale + bias
    m_new = jnp.maximum(m_ref[...], jnp.max(s, axis=-1, keepdims=True))
    alpha = jnp.exp(m_ref[...] - m_new)
    p = jnp.exp(s - m_new)
    l_ref[...] = alpha * l_ref[...] + jnp.sum(p, axis=-1, keepdims=True)
    m_ref[...] = m_new
    pv_grp = []
    for gi in range(n_grp):
        pg = p[gi * SUBLANE:(gi + 1) * SUBLANE, :]
        pv = jnp.zeros((SUBLANE, LANE), F32)
        for hh in range(SUBLANE):
            h = gi * SUBLANE + hh
            pv = pv + _dot(jnp.where(sub == hh, pg, 0.0), vp_ref[:, h * LANE:(h + 1) * LANE])
        pv_grp.append(pv)
    acc_ref[...] = alpha * acc_ref[...] + jnp.concatenate(pv_grp, axis=0)

    @pl.when(p_id == n_pages - 1)
    def _():
        s_new = jnp.sum(q_ref[0] * kn_ref[0], axis=-1, keepdims=True) * scale
        m_fin = jnp.maximum(m_ref[...], s_new)
        al = jnp.exp(m_ref[...] - m_fin)
        p_new = jnp.exp(s_new - m_fin)
        l_fin = al * l_ref[...] + p_new
        o_ref[0] = ((al * acc_ref[...] + p_new * vn_ref[0]) / l_fin).astype(o_ref.dtype)


def _fox_decode(q, k_new, v_new, lf_new, cache_k, cache_v, cache_lft, layer, page_table, head_dim):
    Bd, H, hd = q.shape
    assert hd == LANE and H % SUBLANE == 0
    P, D = cache_k.shape[2], cache_k.shape[3]
    n_pages = page_table.shape[1]
    row = lambda: pl.BlockSpec((1, H, hd), lambda b, p, pt: (b, 0, 0))
    page_idx = lambda b, p, pt: (layer, pt[b, n_pages - 1 - p], 0, 0)
    gs = pltpu.PrefetchScalarGridSpec(
        num_scalar_prefetch=1,
        grid=(Bd, n_pages),
        in_specs=[row(), row(), row(),
                  pl.BlockSpec((1, H, 1), lambda b, p, pt: (b, 0, 0)),
                  pl.BlockSpec((None, None, P, D), page_idx),
                  pl.BlockSpec((None, None, P, D), page_idx),
                  pl.BlockSpec((None, None, H, P), page_idx)],
        out_specs=pl.BlockSpec((1, H, hd), lambda b, p, pt: (b, 0, 0)),
        scratch_shapes=[pltpu.VMEM((H, SUBLANE, LANE), F32),
                        pltpu.VMEM((H, 1), F32), pltpu.VMEM((H, 1), F32),
                        pltpu.VMEM((H, LANE), F32), pltpu.VMEM((H, 1), F32)])
    est = 4 * _nbytes((P, D), F32) + 2 * MIB
    return pl.pallas_call(
        functools.partial(_fox_decode_body, H, head_dim ** -0.5),
        grid_spec=gs,
        out_shape=jax.ShapeDtypeStruct((Bd, H, hd), BF16),
        compiler_params=_params(("parallel", "arbitrary"), est),
        name="fox_decode",
    )(page_table, q, k_new, v_new, lf_new, cache_k, cache_v, cache_lft)


def _head_rms(acc, g_row, head_dim):
    tm, tn = acc.shape
    parts = []
    for h in range(tn // head_dim):
        blk = acc[:, h * head_dim:(h + 1) * head_dim]
        blk = blk * lax.rsqrt(jnp.mean(blk * blk, axis=-1, keepdims=True) + NORM_EPS)
        parts.append(blk * g_row)
    return parts[0] if len(parts) == 1 else jnp.concatenate(parts, axis=-1)


def _qkv_proj(h, w_qkvf, layer, qk_gain, head_dim):
    _, M, D = h.shape
    tm = _row_tile(M)
    tn = _div_tile(D, 512 if M > 64 else 1024)
    nj = D // tn

    def epi(acc, gq):
        return _head_rms(acc, gq, head_dim)

    (qk,) = _linear(
        (M // tm, 2 * nj), h,
        pl.BlockSpec((1, tm, D), lambda i, j: (0, i, 0)),
        [w_qkvf], pl.BlockSpec((1, D, tn), lambda i, j: (layer, 0, j)),
        [(qk_gain, pl.BlockSpec((1, 1, head_dim), lambda i, j: (j // nj, 0, 0)))],
        [(jax.ShapeDtypeStruct((2, M, D), F32),
          pl.BlockSpec((1, tm, tn), lambda i, j: (j // nj, i, j % nj)))],
        epi, ("parallel", "arbitrary"), name="qk_proj")
    (v,) = _linear(
        (M // tm, nj), h,
        pl.BlockSpec((1, tm, D), lambda i, j: (0, i, 0)),
        [w_qkvf], pl.BlockSpec((1, D, tn), lambda i, j: (layer, 0, 2 * nj + j)),
        [],
        [(jax.ShapeDtypeStruct((1, M, D), F32), pl.BlockSpec((1, tm, tn), lambda i, j: (0, i, j)))],
        lambda acc: acc, ("parallel", "arbitrary"), name="v_proj")
    return qk, v


def _swiglu_epi(g, u):
    return (g * jax.nn.sigmoid(g)) * u


def _col_segments(F, tn):
    assert F % LANE == 0 and tn % LANE == 0
    n = F // tn
    segs = [(0, tn, n)] if n else []
    if F - n * tn:
        segs.append((n * tn, F - n * tn, 1))
    return segs


def _ffn_up_dense(h, w_gate, w_up, layer):
    _, M, D = h.shape
    F = w_gate.shape[2]
    tm = _row_tile(M)
    E = pl.Element
    segs = _col_segments(F, 256 if M > 64 else 512)
    out = jnp.zeros((M, F), BF16) if len(segs) > 1 else None
    for c0, tn, n_j in segs:
        off = lambda j: pl.multiple_of(c0 + j * tn, LANE)
        (out,) = _linear(
            (M // tm, n_j), h,
            pl.BlockSpec((1, tm, D), lambda i, j: (0, i, 0)),
            [w_gate, w_up], pl.BlockSpec((None, E(D), E(tn)), lambda i, j: (layer, 0, off(j))),
            [],
            [(jax.ShapeDtypeStruct((M, F), BF16),
              pl.BlockSpec((E(tm), E(tn)), lambda i, j: (i * tm, off(j))))],
            _swiglu_epi, ("parallel", "arbitrary"), into=out, name="ffn_up")
    return out


def _moe_up_sorted(a_sorted, w_gate, w_up, layer_base, n_exp, tile_expert, tm):
    _, Mp, D = a_sorted.shape
    F = w_gate.shape[2]
    E = pl.Element
    segs = _col_segments(F, 512)
    out = jnp.zeros((Mp, F), BF16) if len(segs) > 1 else None
    for c0, tn, n_j in segs:
        off = lambda j: pl.multiple_of(c0 + j * tn, LANE)
        (out,) = _linear(
            (n_j, Mp // tm), a_sorted,
            pl.BlockSpec((1, tm, D), lambda j, s, te: (0, s, 0)),
            [w_gate, w_up],
            pl.BlockSpec((None, E(D), E(tn)), lambda j, s, te: (layer_base + te[s] % n_exp, 0, off(j))),
            [],
            [(jax.ShapeDtypeStruct((Mp, F), BF16),
              pl.BlockSpec((E(tm), E(tn)), lambda j, s, te: (s * tm, off(j))))],
            _swiglu_epi, ("parallel", "arbitrary"), prefetch=tile_expert, into=out,
            is_padding=lambda te: te[pl.program_id(1)] >= n_exp, name="moe_up")
    return out


def _moe_down_sorted(a_sorted, w_down, layer_base, n_exp, tile_expert, pair_gate, tm):
    _, Mp, F = a_sorted.shape
    D = w_down.shape[2]
    tn = _div_tile(D, 512)
    (out,) = _linear(
        (D // tn, Mp // tm), a_sorted,
        pl.BlockSpec((1, tm, F), lambda j, s, te: (0, s, 0)),
        [w_down], pl.BlockSpec((1, F, tn), lambda j, s, te: (layer_base + te[s] % n_exp, 0, j)),
        [(pair_gate, pl.BlockSpec((tm, 1), lambda j, s, te: (s, 0)))],
        [(jax.ShapeDtypeStruct((Mp, D), F32), pl.BlockSpec((tm, tn), lambda j, s, te: (s, j)))],
        lambda acc, pg: acc * pg, ("parallel", "arbitrary"), prefetch=tile_expert,
        is_padding=lambda te: te[pl.program_id(1)] >= n_exp, name="moe_down")
    return out


def _row_copy(src_ref, row, dst_ref, sem):
    return pltpu.make_async_copy(src_ref.at[pl.ds(row, 1), :], dst_ref, sem)


def _gather_rows_body(n_rows, idx_ref, src_ref, out_ref, buf_ref, sem_ref):
    s = pl.program_id(0)
    slot = s % 2

    def issue(step, to_slot):
        def one(r, carry):
            _row_copy(src_ref, idx_ref[step * n_rows + r], buf_ref.at[to_slot, pl.ds(r, 1), :],
                      sem_ref.at[to_slot]).start()
            return carry
        lax.fori_loop(0, n_rows, one, 0)

    @pl.when(s == 0)
    def _():
        issue(0, 0)

    @pl.when(s + 1 < pl.num_programs(0))
    def _():
        issue(s + 1, 1 - slot)

    def wait_one(r, carry):
        _row_copy(src_ref, 0, buf_ref.at[slot, pl.ds(r, 1), :], sem_ref.at[slot]).wait()
        return carry
    lax.fori_loop(0, n_rows, wait_one, 0)
    out_ref[...] = buf_ref[slot].astype(out_ref.dtype)


def _gather_rows(src, idx, n_rows=256):
    M, D = src.shape
    n_out = idx.shape[0]
    assert n_out % n_rows == 0
    gs = pltpu.PrefetchScalarGridSpec(
        num_scalar_prefetch=1, grid=(n_out // n_rows,),
        in_specs=[pl.BlockSpec(memory_space=pl.ANY)],
        out_specs=pl.BlockSpec((n_rows, D), lambda s, ix: (s, 0)),
        scratch_shapes=[pltpu.VMEM((2, n_rows, D), F32), pltpu.SemaphoreType.DMA((2,))])
    est = 2 * _nbytes((n_rows, D), F32) + 3 * _nbytes((n_rows, D), BF16)
    return pl.pallas_call(
        functools.partial(_gather_rows_body, n_rows),
        grid_spec=gs,
        out_shape=jax.ShapeDtypeStruct((n_out, D), BF16),
        compiler_params=_params(("arbitrary",), est),
        name="moe_gather",
    )(idx, src)


def _combine_body(n_tok, pos_ref, y_ref, x_ref, gate_ref, out_ref, buf_ref, sem_ref):
    n_inner = pl.num_programs(1)
    s = pl.program_id(0) * n_inner + pl.program_id(1)
    n_steps = pl.num_programs(0) * n_inner
    slot = s % 2

    def issue(step, to_slot):
        def one(t, carry):
            for k in range(TOP_K):
                _row_copy(y_ref, pos_ref[(step * n_tok + t) * TOP_K + k],
                          buf_ref.at[to_slot, k, pl.ds(t, 1), :], sem_ref.at[to_slot]).start()
            return carry
        lax.fori_loop(0, n_tok, one, 0)

    @pl.when(s == 0)
    def _():
        issue(0, 0)

    @pl.when(s + 1 < n_steps)
    def _():
        issue(s + 1, 1 - slot)

    def wait_one(t, carry):
        for k in range(TOP_K):
            _row_copy(y_ref, 0, buf_ref.at[slot, k, pl.ds(t, 1), :], sem_ref.at[slot]).wait()
        return carry
    lax.fori_loop(0, n_tok, wait_one, 0)
    acc = buf_ref[slot, 0]
    for k in range(1, TOP_K):
        acc = acc + buf_ref[slot, k]
    out_ref[0] = x_ref[0] + gate_ref[0] * acc


def _moe_combine(y_sorted, pos, x, gate, n_tok=128):
    Bx, R, D = x.shape
    n_tok = _div_tile(R, n_tok)
    gs = pltpu.PrefetchScalarGridSpec(
        num_scalar_prefetch=1, grid=(Bx, R // n_tok),
        in_specs=[pl.BlockSpec(memory_space=pl.ANY),
                  pl.BlockSpec((1, n_tok, D), lambda b, s, ps: (b, s, 0)),
                  pl.BlockSpec((1, 1, D), lambda b, s, ps: (b, 0, 0))],
        out_specs=pl.BlockSpec((1, n_tok, D), lambda b, s, ps: (b, s, 0)),
        scratch_shapes=[pltpu.VMEM((2, TOP_K, n_tok, D), F32), pltpu.SemaphoreType.DMA((2,))])
    est = (2 * TOP_K + 6) * _nbytes((n_tok, D), F32)
    return pl.pallas_call(
        functools.partial(_combine_body, n_tok),
        grid_spec=gs,
        out_shape=jax.ShapeDtypeStruct((Bx, R, D), F32),
        compiler_params=_params(("arbitrary", "arbitrary"), est),
        name="moe_combine",
    )(pos, y_sorted, x, gate)


def _moe_up_all(h, w_gate, w_up, layer_base, n_exp):
    _, R, D = h.shape
    F = w_gate.shape[2]
    E = pl.Element
    segs = _col_segments(F, 512)
    out = jnp.zeros((n_exp, R, F), BF16) if len(segs) > 1 else None
    for c0, tn, n_j in segs:
        off = lambda j: pl.multiple_of(c0 + j * tn, LANE)
        (out,) = _linear(
            (n_exp, n_j), h,
            pl.BlockSpec((1, R, D), lambda e, j: (0, 0, 0)),
            [w_gate, w_up], pl.BlockSpec((None, E(D), E(tn)), lambda e, j: (layer_base + e, 0, off(j))),
            [],
            [(jax.ShapeDtypeStruct((n_exp, R, F), BF16),
              pl.BlockSpec((None, E(R), E(tn)), lambda e, j: (e, 0, off(j))))],
            _swiglu_epi, ("parallel", "arbitrary"), into=out, name="moe_up_all")
    return out


def _moe_down_all_body(a_ref, w_ref, ge_ref, x_ref, gate_ref, o_ref, acc_ref):
    e = pl.program_id(1)

    @pl.when(e == 0)
    def _():
        acc_ref[...] = jnp.zeros(acc_ref.shape, F32)

    acc_ref[...] += ge_ref[0] * _dot(a_ref[0], w_ref[0])

    @pl.when(e == pl.num_programs(1) - 1)
    def _():
        o_ref[0] = x_ref[0] + gate_ref[0] * acc_ref[...]


def _moe_down_all(a, w_down, layer_base, gates_t, x, gate):
    n_exp, R, F = a.shape
    D = w_down.shape[2]
    tn = _div_tile(D, 512)
    est = 2 * (_nbytes((F, tn), F32) + _nbytes((R, F), BF16)) + _nbytes((F, tn), BF16) + 8 * _nbytes((R, tn), F32)
    return pl.pallas_call(
        _moe_down_all_body,
        grid=(D // tn, n_exp),
        in_specs=[pl.BlockSpec((1, R, F), lambda j, e: (e, 0, 0)),
                  pl.BlockSpec((1, F, tn), lambda j, e: (layer_base + e, 0, j)),
                  pl.BlockSpec((1, R, 1), lambda j, e: (e, 0, 0)),
                  pl.BlockSpec((1, R, tn), lambda j, e: (0, 0, j)),
                  pl.BlockSpec((1, R, tn), lambda j, e: (0, 0, j))],
        out_specs=pl.BlockSpec((1, R, tn), lambda j, e: (0, 0, j)),
        out_shape=jax.ShapeDtypeStruct((1, R, D), F32),
        scratch_shapes=[pltpu.VMEM((R, tn), F32)],
        compiler_params=_params(("parallel", "arbitrary"), est),
        name="moe_down_all",
    )(a, w_down, gates_t, x, gate)


def _route_metadata(gates, selmask, n_exp, tm):
    M = gates.shape[0]
    n_tiles = (M * TOP_K) // tm + n_exp
    sel = selmask > 0.5
    seli = sel.astype(jnp.int32)
    rank = jnp.cumsum(seli, axis=0) - seli
    counts = jnp.sum(seli, axis=0)
    tiles_per = (counts + tm - 1) // tm
    tile_start = jnp.cumsum(tiles_per) - tiles_per
    pos = tile_start[None, :] * tm + rank
    tile_ids = jnp.arange(n_tiles, dtype=jnp.int32)
    tile_expert = jnp.clip(jnp.searchsorted(jnp.cumsum(tiles_per), tile_ids, side="right"),
                           0, n_exp - 1).astype(jnp.int32)
    tile_expert = jnp.where(tile_ids < jnp.sum(tiles_per), tile_expert, tile_expert + n_exp)
    flat_pos = jnp.where(sel, pos, n_tiles * tm).reshape(-1)
    pair = jnp.zeros((n_tiles * tm,), jnp.int32).at[flat_pos].set(
        jnp.arange(1, M * n_exp + 1, dtype=jnp.int32), mode="drop")
    pair_idx = jnp.maximum(pair - 1, 0)
    row_src = pair_idx // n_exp
    row_gate = jnp.where(pair > 0, gates.reshape(-1)[pair_idx], 0.0)
    order = jnp.argsort(jnp.where(sel, 0, 1), axis=1, stable=True)[:, :TOP_K]
    tok_pos = jnp.take_along_axis(pos, order, axis=1).astype(jnp.int32).reshape(-1)
    return row_src, tile_expert, row_gate.reshape(-1, 1), tok_pos


def _trunk(x, mods, rows_per_batch, seq, P, shift_in, wkv_in, paged):
    Bx, R, D = x.shape
    M = Bx * R
    depth = P["norm_g"].shape[0]
    n_rwkv_heads, n_state = P["state_dims"]
    head_dim = P["attn_q_g"].shape[-1]
    n_attn_heads = D // head_dim
    n_exp = P["moe_w_router"].shape[-1]
    lo_w, lo_a, lo_g = P["rwkv_w1"].shape[-1], P["rwkv_a1"].shape[-1], P["rwkv_g1"].shape[-1]
    lo_v = P["rwkv_v1"].shape[-1]
    n_seqs = Bx if seq else R
    T = R if seq else 1
    flat = lambda t: t.reshape(1, M, t.shape[-1])

    shifts, wkvs, ks, vs, lfs = [], [], [], [], []
    v_first = None
    for i in range(depth):
        j = i // 2
        shift, scale, gate = mods[(i, 0)]
        g_row = P["norm_g"][i, 0][None, :]
        if i % 2 == 0:
            mu = P["rwkv_mu"][j][jnp.array([0, 2, 3, 1, 4, 5])]
            if seq:
                sp = jnp.zeros((Bx, 1, D), F32) if shift_in is None else shift_in[j][:, None, :]
            else:
                sp = shift_in[j][None]
            lerps, h_keep = _prep_rwkv(x, g_row, shift, scale, mu, sp, seq)
            lerps = lerps.reshape(N_LERP, M, D)
            shifts.append(h_keep.reshape(n_seqs, D))
            n_l = P["rwkv_w_rkv"].shape[0]
            w_rkv = P["rwkv_w_rkv"].reshape(n_l * 3, D, D)
            tm = _row_tile(M)
            tn = _div_tile(D, 512 if M > 64 else 1024)
            (rkv,) = _linear(
                (3, M // tm, D // tn), lerps,
                pl.BlockSpec((1, tm, D), lambda s, a, b: (s, a, 0)),
                [w_rkv], pl.BlockSpec((1, D, tn), lambda s, a, b: (3 * j + s, 0, b)),
                [],
                [(jax.ShapeDtypeStruct((3, M, D), F32), pl.BlockSpec((1, tm, tn), lambda s, a, b: (s, a, b)))],
                lambda acc: acc, ("parallel", "parallel", "arbitrary"), name="rkv_proj")
            row = lambda name: P[name][j][None, :]
            w_mid = _proj(lerps, P["rwkv_w1"], j, a_lead=3, epilogue=jnp.tanh, out_dtype=BF16, name="lora_w1")
            w_pre = _proj(w_mid[None], P["rwkv_w2"], j, extras=[row("rwkv_w0")],
                          epilogue=lambda acc, b: acc + b, name="lora_w2")
            a_mid = _proj(lerps, P["rwkv_a1"], j, a_lead=4, out_dtype=BF16, name="lora_a1")
            a_gate = _proj(a_mid[None], P["rwkv_a2"], j, extras=[row("rwkv_a0")],
                           epilogue=lambda acc, b: jax.nn.sigmoid(acc + b), name="lora_a2")
            g_mid = _proj(lerps, P["rwkv_g1"], j, a_lead=5, epilogue=jax.nn.sigmoid, out_dtype=BF16,
                          name="lora_g1")
            g_out = _proj(g_mid[None], P["rwkv_g2"], j, name="lora_g2")
            if j == 0:
                vres = None
                v_first = rkv[2]
            else:
                v_mid = _proj(lerps, P["rwkv_v1"], j - 1, a_lead=2, out_dtype=BF16, name="lora_v1")
                v_gate = _proj(v_mid[None], P["rwkv_v2"], j - 1, extras=[P["rwkv_v0"][j - 1][None, :]],
                               epilogue=lambda acc, b: jax.nn.sigmoid(acc + b), name="lora_v2")
                vres = (v_gate, v_first)
            s0 = (jnp.zeros((n_seqs, n_rwkv_heads, n_state, n_state), F32) if wkv_in is None else wkv_in[j])
            vecs = (row("rwkv_k_k"), row("rwkv_k_a"), P["rwkv_r_k"][j].reshape(1, D),
                    row("rwkv_ln_w"), row("rwkv_ln_b"))
            if seq:
                shp = lambda t: t.reshape(n_seqs, T, D)
                z, s_bd = _rwkv_scan(rkv.reshape(3, n_seqs, T, D), shp(w_pre), shp(a_gate), shp(g_out),
                                     None if vres is None else tuple(shp(t) for t in vres),
                                     vecs, _to_block_diag(s0), SCAN_CHUNK)
                wkvs.append(_from_block_diag(s_bd, n_state))
                z = z.reshape(1, M, D)
            else:
                hs = lambda t: t.reshape(n_seqs, n_rwkv_heads, 1, n_state)
                seqs = [hs(rkv[0]), hs(rkv[1]), hs(rkv[2]), hs(w_pre), hs(a_gate), hs(g_out)]
                if vres is not None:
                    seqs += [hs(vres[0]), hs(vres[1])]
                z, s_new = _rwkv_step(seqs, [t.reshape(n_rwkv_heads, 1, n_state) for t in vecs], s0,
                                      STEP_MXU_ROUND)
                wkvs.append(s_new)
                z = z.reshape(1, M, D)
            x = _proj_residual(z, P["rwkv_w_o"], j, x.reshape(M, D), gate, rows_per_batch,
                               name="rwkv_out").reshape(Bx, R, D)
        else:
            h = _prep_plain(x, g_row, shift, scale)
            qk, v_att = _qkv_proj(flat(h), P["attn_w_qkvf"], j,
                                  jnp.stack([P["attn_q_g"][j], P["attn_k_g"][j]])[:, None, :], head_dim)
            w_f = jnp.pad(P["attn_w_qkvf"][j][:, 3 * D:], ((0, 0), (0, LANE - n_attn_heads)))[None]
            b_f = jnp.pad(P["attn_b_f"][j], (0, LANE - n_attn_heads))[None, :]
            logf = _proj(flat(h), w_f, 0, extras=[b_f], epilogue=lambda acc, b: -_softplus(-(acc + b)),
                         name="logf_proj")
            ks.append(qk[1].reshape(n_seqs, T, n_attn_heads, head_dim))
            vs.append(v_att[0].reshape(n_seqs, T, n_attn_heads, head_dim))
            lfs.append(logf[:, :n_attn_heads].reshape(n_seqs, T, n_attn_heads))
            if not paged:
                f = _cumsum_rows(logf.reshape(n_seqs, T, LANE))[:, :, :n_attn_heads]
                f_bht = jnp.transpose(f, (0, 2, 1))
                o = _fox_prefill(qk.reshape(2, n_seqs, T, D), v_att.reshape(1, n_seqs, T, D),
                                 f_bht[:, :, None, :], f_bht[:, :, :, None], head_dim)
            else:
                hv = lambda t: t.reshape(n_seqs, n_attn_heads, head_dim)
                o = _fox_decode(hv(qk[0]), hv(qk[1]), hv(v_att[0]),
                                logf[:, :n_attn_heads].reshape(n_seqs, n_attn_heads, 1),
                                P["cache_k"], P["cache_v"], P["cache_lft"], j, P["page_table"], head_dim)
            x = _proj_residual(o.reshape(1, M, D), P["attn_w_o"], j, x.reshape(M, D), gate, rows_per_batch,
                               name="attn_out").reshape(Bx, R, D)

        shift, scale, gate = mods[(i, 1)]
        g_row = P["norm_g"][i, 1][None, :]
        if i % 2 == 0:
            h = _prep_plain(x, g_row, shift, scale)
            hid = _ffn_up_dense(flat(h), P["ffn_w_gate"], P["ffn_w_up"], j)
            F = hid.shape[1]
            part = _proj_residual(hid[None], P["ffn_w_down"], j, None, None, rows_per_batch,
                                  k_blk=0, k_size=F // 2, final=False, name="ffn_down0")
            x = _proj_residual(hid[None], P["ffn_w_down"], j, x.reshape(M, D), gate, rows_per_batch,
                               k_blk=1, k_size=F // 2, prev=part, name="ffn_down1").reshape(Bx, R, D)
        else:
            w_router = jnp.pad(P["moe_w_router"][j], ((0, 0), (0, LANE - n_exp)))
            n_l = P["moe_w_gate"].shape[0]
            wg = P["moe_w_gate"].reshape((n_l * n_exp,) + P["moe_w_gate"].shape[2:])
            wu = P["moe_w_up"].reshape((n_l * n_exp,) + P["moe_w_up"].shape[2:])
            wd = P["moe_w_down"].reshape((n_l * n_exp,) + P["moe_w_down"].shape[2:])
            if seq:
                h32, gates, selmask = _prep_moe(x, g_row, shift, scale, w_router, n_exp, F32)
                tm = _div_tile(M, 512)
                row_src, tile_expert, row_gate, tok_pos = _route_metadata(
                    gates.reshape(M, LANE)[:, :n_exp], selmask.reshape(M, LANE)[:, :n_exp], n_exp, tm)
                a_sorted = _gather_rows(h32.reshape(M, D), row_src)
                hid = _moe_up_sorted(a_sorted[None], wg, wu, j * n_exp, n_exp, tile_expert, tm)
                y = _moe_down_sorted(hid[None], wd, j * n_exp, n_exp, tile_expert, row_gate, tm)
                x = _moe_combine(y, tok_pos, x, gate)
            else:
                hb, gates, _ = _prep_moe(x, g_row, shift, scale, w_router, n_exp, BF16)
                hid = _moe_up_all(hb, wg, wu, j * n_exp, n_exp)
                gates_t = jnp.transpose(gates.reshape(M, LANE)[:, :n_exp])[:, :, None]
                x = _moe_down_all(hid, wd, j * n_exp, gates_t, x, gate)
    return x, ks, vs, lfs, shifts, wkvs


def kernel(x_prompt, x_sample, c_prompt, c_sample, cache_k, cache_v, cache_logf, page_table, state_shift, state_wkv, ada_w, ada_b, norm_g, rwkv_mu, rwkv_w_rkv, rwkv_w0, rwkv_w1, rwkv_w2, rwkv_a0, rwkv_a1, rwkv_a2, rwkv_v0, rwkv_v1, rwkv_v2, rwkv_g1, rwkv_g2, rwkv_k_k, rwkv_k_a, rwkv_r_k, rwkv_ln_w, rwkv_ln_b, rwkv_w_o, attn_w_qkvf, attn_b_f, attn_q_g, attn_k_g, attn_w_o, ffn_w_gate, ffn_w_up, ffn_w_down, moe_w_router, moe_w_gate, moe_w_up, moe_w_down):
    B, T, D = x_prompt.shape
    Bd = x_sample.shape[0]
    depth = ada_w.shape[0]
    n_layers_attn, n_pool, page, n_attn_heads, head_dim = cache_k.shape

    P = dict(
        norm_g=norm_g, rwkv_mu=rwkv_mu, rwkv_w_rkv=rwkv_w_rkv, rwkv_w0=rwkv_w0, rwkv_w1=rwkv_w1,
        rwkv_w2=rwkv_w2, rwkv_a0=rwkv_a0, rwkv_a1=rwkv_a1, rwkv_a2=rwkv_a2, rwkv_v0=rwkv_v0,
        rwkv_v1=rwkv_v1, rwkv_v2=rwkv_v2, rwkv_g1=rwkv_g1, rwkv_g2=rwkv_g2, rwkv_k_k=rwkv_k_k,
        rwkv_k_a=rwkv_k_a, rwkv_r_k=rwkv_r_k, rwkv_ln_w=rwkv_ln_w, rwkv_ln_b=rwkv_ln_b,
        rwkv_w_o=rwkv_w_o, attn_w_qkvf=attn_w_qkvf, attn_b_f=attn_b_f, attn_q_g=attn_q_g,
        attn_k_g=attn_k_g, attn_w_o=attn_w_o, ffn_w_gate=ffn_w_gate, ffn_w_up=ffn_w_up,
        ffn_w_down=ffn_w_down, moe_w_router=moe_w_router, moe_w_gate=moe_w_gate, moe_w_up=moe_w_up,
        moe_w_down=moe_w_down,
        state_dims=(state_wkv.shape[2], state_wkv.shape[3]),
        cache_k=cache_k.reshape(n_layers_attn, n_pool, page, n_attn_heads * head_dim),
        cache_v=cache_v.reshape(n_layers_attn, n_pool, page, n_attn_heads * head_dim),
        cache_lft=jnp.swapaxes(cache_logf, 2, 3),
        page_table=page_table,
    )

    n_rows = B + Bd
    rows_pad = -(-n_rows // SUBLANE) * SUBLANE
    c_all = jnp.pad(jnp.concatenate([c_prompt, c_sample], axis=0), ((0, rows_pad - n_rows), (0, 0)))
    mods_all = _ada_mods(c_all, ada_w.reshape(depth * 2, D, 3 * D), ada_b.reshape(depth * 2, 1, 3 * D))

    def split_mods(r0, r1, per_row):
        out = {}
        for i in range(depth):
            for s in range(2):
                m = mods_all[i * 2 + s, r0:r1]
                parts = [m[:, k * D:(k + 1) * D] for k in range(3)]
                out[(i, s)] = tuple(p[None] if per_row else p[:, None, :] for p in parts)
        return out

    yp, ks, vs, lfs, shifts, wkvs = _trunk(
        x_prompt, split_mods(0, B, False), T, True, P, None, None, False)
    ys, ks2, vs2, lfs2, shifts2, wkvs2 = _trunk(
        x_sample.reshape(1, Bd, D), split_mods(B, B + Bd, True), 1, False, P, state_shift, state_wkv, True)
    return (yp, ys.reshape(Bd, 1, D),
            jnp.stack(ks), jnp.stack(vs), jnp.stack(lfs),
            jnp.stack(ks2), jnp.stack(vs2), jnp.stack(lfs2),
            jnp.stack(shifts), jnp.stack(wkvs), jnp.stack(shifts2), jnp.stack(wkvs2))
```

```python
import functools
import math

import jax
import jax.numpy as jnp
from jax import lax
from jax.experimental import pallas as pl
from jax.experimental.pallas import tpu as pltpu

F32 = jnp.float32
BF16 = jnp.bfloat16

NORM_EPS = 1e-6
GN_EPS = 64e-5
NEG_INF = -1e30
TOP_K = 2
N_LERP = 6

LANE = 128
SUBLANE = 8
VMEM_BYTES = 64 * 1024 * 1024
VMEM_CAP = VMEM_BYTES - 6 * 1024 * 1024
MIB = 1024 * 1024

HI = lax.Precision.HIGHEST

SCAN_CHUNK = 64
SCAN_PAIRS = 16
SCAN_ROWS = 256
STEP_MXU_ROUND = True


def _params(sem, est_bytes):
    limit = min(max(int(est_bytes) + 4 * MIB, 32 * MIB), VMEM_CAP)
    return pltpu.CompilerParams(dimension_semantics=sem, vmem_limit_bytes=limit)


def _nbytes(shape, dtype):
    return math.prod(shape) * jnp.dtype(dtype).itemsize


def _div_tile(n, pref):
    if n <= pref:
        return n
    t = pref
    while n % t:
        t //= 2
    return t


def _dot(a, b):
    return jnp.dot(a.astype(BF16), b.astype(BF16), preferred_element_type=F32)


def _dot_nt(a, b):
    return lax.dot_general(a.astype(BF16), b.astype(BF16), (((1,), (1,)), ((), ())),
                           preferred_element_type=F32)


def _dot_tn(a, b):
    return lax.dot_general(a.astype(BF16), b.astype(BF16), (((0,), (0,)), ((), ())),
                           preferred_element_type=F32)


def _softplus(z):
    return jnp.maximum(z, 0.0) + jnp.log1p(jnp.exp(-jnp.abs(z)))


def _rms_mod(x, g, shift, scale):
    y = x * lax.rsqrt(jnp.mean(x * x, axis=-1, keepdims=True) + NORM_EPS)
    return (y * g) * (1.0 + scale) + shift


def _ada_body(c_ref, w_ref, b_ref, o_ref):
    c = c_ref[...]
    a = c * jax.nn.sigmoid(c)
    o_ref[0] = _dot(a, w_ref[0]) + b_ref[0]


def _ada_mods(c_all, ada_w, ada_b):
    S, D, N = ada_w.shape
    Rp = c_all.shape[0]
    tn = _div_tile(N, 512)
    est = 2 * (_nbytes((D, tn), F32) + _nbytes((Rp, D), F32) + 2 * _nbytes((Rp, tn), F32))
    return pl.pallas_call(
        _ada_body,
        grid=(S, N // tn),
        in_specs=[pl.BlockSpec((Rp, D), lambda s, j: (0, 0)),
                  pl.BlockSpec((1, D, tn), lambda s, j: (s, 0, j)),
                  pl.BlockSpec((1, 1, tn), lambda s, j: (s, 0, j))],
        out_specs=pl.BlockSpec((1, Rp, tn), lambda s, j: (s, 0, j)),
        out_shape=jax.ShapeDtypeStruct((S, Rp, N), F32),
        compiler_params=_params(("parallel", "parallel"), est),
        name="ada_mods",
    )(c_all, ada_w, ada_b)


def _mod_spec(mod, R, tr):
    D = mod.shape[-1]
    if mod.shape[1] == 1:
        return pl.BlockSpec((1, 1, D), lambda b, r: (b, 0, 0))
    assert mod.shape[1] == R
    return pl.BlockSpec((1, tr, D), lambda b, r: (b, r, 0))


def _prep_plain_body(x_ref, g_ref, sh_ref, sc_ref, h_ref):
    h_ref[0] = _rms_mod(x_ref[0], g_ref[...], sh_ref[0], sc_ref[0]).astype(h_ref.dtype)


def _prep_plain(x, g, shift, scale):
    Bx, R, D = x.shape
    tr = _div_tile(R, 256)
    est = 2 * (_nbytes((tr, D), F32) * 3 + _nbytes((tr, D), BF16))
    return pl.pallas_call(
        _prep_plain_body,
        grid=(Bx, R // tr),
        in_specs=[pl.BlockSpec((1, tr, D), lambda b, r: (b, r, 0)),
                  pl.BlockSpec((1, D), lambda b, r: (0, 0)),
                  _mod_spec(shift, R, tr), _mod_spec(scale, R, tr)],
        out_specs=pl.BlockSpec((1, tr, D), lambda b, r: (b, r, 0)),
        out_shape=jax.ShapeDtypeStruct((Bx, R, D), BF16),
        compiler_params=_params(("parallel", "parallel"), est),
        name="prep_plain",
    )(x, g, shift, scale)


def _prep_moe_body(n_exp, x_ref, g_ref, sh_ref, sc_ref, wr_ref, h_ref, gates_ref, sel_ref):
    h = _rms_mod(x_ref[0], g_ref[...], sh_ref[0], sc_ref[0])
    h_ref[0] = h.astype(h_ref.dtype)
    logits = _dot(h, wr_ref[...])
    lane = lax.broadcasted_iota(jnp.int32, logits.shape, 1)
    logits = jnp.where(lane < n_exp, logits, NEG_INF)
    e = jnp.exp(logits - jnp.max(logits, axis=-1, keepdims=True))
    p = e / jnp.sum(e, axis=-1, keepdims=True)
    p = jnp.where(lane < n_exp, p, -1.0)
    lane_f = lane.astype(F32)
    m1 = jnp.max(p, axis=-1, keepdims=True)
    i1 = jnp.min(jnp.where(p == m1, lane_f, float(LANE)), axis=-1, keepdims=True)
    p2 = jnp.where(lane_f == i1, -1.0, p)
    m2 = jnp.max(p2, axis=-1, keepdims=True)
    i2 = jnp.min(jnp.where(p2 == m2, lane_f, float(LANE)), axis=-1, keepdims=True)
    tot = m1 + m2
    gates_ref[0] = jnp.where(lane_f == i1, m1 / tot, jnp.where(lane_f == i2, m2 / tot, 0.0))
    sel_ref[0] = jnp.where((lane_f == i1) | (lane_f == i2), 1.0, 0.0)


def _prep_moe(x, g, shift, scale, w_router_pad, n_exp, h_dtype):
    Bx, R, D = x.shape
    tr = _div_tile(R, 256)
    est = 2 * (_nbytes((tr, D), F32) * 4 + _nbytes((D, LANE), F32))
    return pl.pallas_call(
        functools.partial(_prep_moe_body, n_exp),
        grid=(Bx, R // tr),
        in_specs=[pl.BlockSpec((1, tr, D), lambda b, r: (b, r, 0)),
                  pl.BlockSpec((1, D), lambda b, r: (0, 0)),
                  _mod_spec(shift, R, tr), _mod_spec(scale, R, tr),
                  pl.BlockSpec((D, LANE), lambda b, r: (0, 0))],
        out_specs=[pl.BlockSpec((1, tr, D), lambda b, r: (b, r, 0)),
                   pl.BlockSpec((1, tr, LANE), lambda b, r: (b, r, 0)),
                   pl.BlockSpec((1, tr, LANE), lambda b, r: (b, r, 0))],
        out_shape=[jax.ShapeDtypeStruct((Bx, R, D), h_dtype),
                   jax.ShapeDtypeStruct((Bx, R, LANE), F32),
                   jax.ShapeDtypeStruct((Bx, R, LANE), F32)],
        compiler_params=_params(("parallel", "parallel"), est),
        name="prep_moe",
    )(x, g, shift, scale, w_router_pad)


def _prep_rwkv_body(seq, x_ref, g_ref, sh_ref, sc_ref, mu_ref, sp_ref, lerp_ref, hl_ref, carry_ref):
    h = _rms_mod(x_ref[0], g_ref[...], sh_ref[0], sc_ref[0])
    tr = h.shape[0]
    if seq:
        @pl.when(pl.program_id(1) == 0)
        def _():
            carry_ref[...] = sp_ref[0]
        row = lax.broadcasted_iota(jnp.int32, h.shape, 0)
        h_prev = jnp.where(row == 0, carry_ref[...], pltpu.roll(h, 1, 0))
        carry_ref[...] = h[tr - 1:tr, :]
        hl_ref[0] = h[tr - 1:tr, :]
    else:
        h_prev = sp_ref[0]
        hl_ref[0] = h
    xx = h_prev - h
    for i in range(N_LERP):
        lerp_ref[i, 0] = (h + xx * mu_ref[i:i + 1, :]).astype(lerp_ref.dtype)


def _prep_rwkv(x, g, shift, scale, mu, shift_prev, seq):
    Bx, R, D = x.shape
    tr = _div_tile(R, 128)
    est = 2 * (_nbytes((tr, D), F32) * 4 + _nbytes((N_LERP, tr, D), BF16)) + _nbytes((N_LERP, D), F32)
    if seq:
        sp_spec = pl.BlockSpec((1, 1, D), lambda b, r: (b, 0, 0))
        hl_spec = pl.BlockSpec((1, 1, D), lambda b, r: (b, 0, 0))
        hl_shape = jax.ShapeDtypeStruct((Bx, 1, D), F32)
    else:
        sp_spec = pl.BlockSpec((1, tr, D), lambda b, r: (b, r, 0))
        hl_spec = pl.BlockSpec((1, tr, D), lambda b, r: (b, r, 0))
        hl_shape = jax.ShapeDtypeStruct((Bx, R, D), F32)
    return pl.pallas_call(
        functools.partial(_prep_rwkv_body, seq),
        grid=(Bx, R // tr),
        in_specs=[pl.BlockSpec((1, tr, D), lambda b, r: (b, r, 0)),
                  pl.BlockSpec((1, D), lambda b, r: (0, 0)),
                  _mod_spec(shift, R, tr), _mod_spec(scale, R, tr),
                  pl.BlockSpec((N_LERP, D), lambda b, r: (0, 0)),
                  sp_spec],
        out_specs=[pl.BlockSpec((N_LERP, 1, tr, D), lambda b, r: (0, b, r, 0)), hl_spec],
        out_shape=[jax.ShapeDtypeStruct((N_LERP, Bx, R, D), BF16), hl_shape],
        scratch_shapes=[pltpu.VMEM((1, D), F32)],
        compiler_params=_params(("parallel", "arbitrary"), est),
        name="prep_rwkv",
    )(x, g, shift, scale, mu, shift_prev)


def _linear_body(n_w, n_ex, n_pf, n_into, epilogue, is_padding, *refs):
    pf_refs, refs = refs[:n_pf], refs[n_pf:]
    a_ref = refs[0]
    w_refs = refs[1:1 + n_w]
    ex_refs = refs[1 + n_w:1 + n_w + n_ex]
    out_refs = refs[1 + n_w + n_ex + n_into:]

    def compute():
        a = a_ref[...].reshape(a_ref.shape[-2:])
        accs = [_dot(a, w[...].reshape(w.shape[-2:])) for w in w_refs]
        exs = [e[...].reshape(e.shape[-2:]) for e in ex_refs]
        res = epilogue(*accs, *exs)
        if not isinstance(res, (tuple, list)):
            res = (res,)
        for o, v in zip(out_refs, res):
            o[...] = v.astype(o.dtype).reshape(o.shape)

    if is_padding is None:
        compute()
    else:
        pad = is_padding(*pf_refs)
        pl.when(jnp.logical_not(pad))(compute)

        @pl.when(pad)
        def _():
            for o in out_refs:
                o[...] = jnp.zeros(o.shape, o.dtype)


def _linear(grid, a, a_spec, ws, w_spec, extras, outs, epilogue, sem, prefetch=None, into=None,
            is_padding=None, name="linear"):
    def blk_bytes(spec, dtype):
        dims = [d.block_size if isinstance(d, pl.Element) else (1 if d is None else d)
                for d in spec.block_shape]
        return _nbytes(dims, dtype)

    est = 2 * blk_bytes(a_spec, a.dtype)
    est += sum(2 * blk_bytes(w_spec, w.dtype) + blk_bytes(w_spec, BF16) for w in ws)
    est += sum(2 * blk_bytes(s, e.dtype) for e, s in extras)
    est += sum(4 * blk_bytes(s, F32) for o, s in outs)
    n_pf = 0 if prefetch is None else 1
    body = functools.partial(_linear_body, len(ws), len(extras), n_pf, 0 if into is None else 1, epilogue,
                             is_padding)
    in_specs = [a_spec] + [w_spec] * len(ws) + [s for _, s in extras]
    out_specs = [s for _, s in outs]
    out_shape = [o for o, _ in outs]
    args = [a] + list(ws) + [e for e, _ in extras]
    aliases = {}
    if into is not None:
        aliases = {n_pf + len(args): 0}
        in_specs.append(pl.BlockSpec(memory_space=pl.ANY))
        args.append(into)
    if prefetch is None:
        call = pl.pallas_call(body, grid=grid, in_specs=in_specs, out_specs=out_specs,
                              out_shape=out_shape, input_output_aliases=aliases,
                              compiler_params=_params(sem, est), name=name)
        res = call(*args)
    else:
        gs = pltpu.PrefetchScalarGridSpec(num_scalar_prefetch=1, grid=grid, in_specs=in_specs,
                                          out_specs=out_specs)
        call = pl.pallas_call(body, grid_spec=gs, out_shape=out_shape, input_output_aliases=aliases,
                              compiler_params=_params(sem, est), name=name)
        res = call(prefetch, *args)
    return res


def _row_tile(M):
    return _div_tile(M, 1024)


def _gate_spec_ij(gate, rows_per_batch, tm, tn):
    if gate.shape[1] == 1:
        return pl.BlockSpec((1, 1, tn), lambda i, j: ((i * tm) // rows_per_batch, 0, j))
    assert gate.shape[0] == 1 and gate.shape[1] == tm
    return pl.BlockSpec((1, tm, tn), lambda i, j: (0, 0, j))


def _proj(a, w, w_lead, *, a_lead=0, n_cols=None, epilogue=None, extras=(), out_dtype=F32, tn_pref=512,
          name="proj"):
    _, M, K = a.shape
    N = w.shape[2] if n_cols is None else n_cols
    tm = _row_tile(M)
    tn = _div_tile(N, tn_pref if M > 64 else 1024)
    ex = []
    for e in extras:
        if e.shape[0] == 1:
            ex.append((e, pl.BlockSpec((1, tn), lambda i, j: (0, j))))
        else:
            ex.append((e, pl.BlockSpec((tm, tn), lambda i, j: (i, j))))
    epi = epilogue if epilogue is not None else (lambda acc: acc)
    (out,) = _linear(
        (M // tm, N // tn), a,
        pl.BlockSpec((1, tm, K), lambda i, j: (a_lead, i, 0)),
        [w], pl.BlockSpec((1, K, tn), lambda i, j: (w_lead, 0, j)),
        ex,
        [(jax.ShapeDtypeStruct((M, N), out_dtype), pl.BlockSpec((tm, tn), lambda i, j: (i, j)))],
        epi, ("parallel", "arbitrary"), name=name)
    return out


def _proj_residual(a, w, w_lead, x, gate, rows_per_batch, *, k_blk=0, k_size=None, prev=None,
                   final=True, name="proj_res"):
    M = a.shape[1]
    K = a.shape[2] if k_size is None else k_size
    N = w.shape[2]
    tm = _row_tile(min(M, rows_per_batch) if rows_per_batch > 1 else M)
    tn = _div_tile(N, 256 if M > 64 else 512)
    tile = lambda arr: (arr, pl.BlockSpec((tm, tn), lambda i, j: (i, j)))
    ex = []
    if prev is not None:
        ex.append(tile(prev))
    if final:
        ex.append(tile(x))
        ex.append((gate, _gate_spec_ij(gate, rows_per_batch, tm, tn)))

    def epi(acc, *e):
        e = list(e)
        if prev is not None:
            acc = acc + e.pop(0)
        if final:
            xv, gv = e
            acc = xv + gv * acc
        return acc

    (out,) = _linear(
        (M // tm, N // tn), a,
        pl.BlockSpec((1, tm, K), lambda i, j: (0, i, k_blk)),
        [w], pl.BlockSpec((1, K, tn), lambda i, j: (w_lead, k_blk, j)),
        ex,
        [(jax.ShapeDtypeStruct((M, N), F32), pl.BlockSpec((tm, tn), lambda i, j: (i, j)))],
        epi, ("parallel", "arbitrary"), name=name)
    return out


def _scan_body(C, n_par, n_chunks, has_vres, *refs):
    if has_vres:
        (r_ref, k_ref, v_ref, w_ref, a_ref, g_ref, vg_ref, vf_ref,
         kk_ref, ka_ref, rk_ref, lnw_ref, lnb_ref, s0_ref, z_ref, st_ref, s_scr) = refs
    else:
        (r_ref, k_ref, v_ref, w_ref, a_ref, g_ref,
         kk_ref, ka_ref, rk_ref, lnw_ref, lnb_ref, s0_ref, z_ref, st_ref, s_scr) = refs
    t_id = pl.program_id(2)

    @pl.when(t_id == 0)
    def _():
        s_scr[...] = s0_ref[0]

    half = LANE // 2
    lane = lax.broadcasted_iota(jnp.int32, (1, LANE), 1)
    m0 = lane < half
    r2 = lax.broadcasted_iota(jnp.int32, (2 * C, 2 * C), 0) % C
    c2 = lax.broadcasted_iota(jnp.int32, (2 * C, 2 * C), 1) % C
    stril = r2 > c2
    tril = r2 >= c2
    tri_c = (lax.broadcasted_iota(jnp.int32, (C, C), 0)
             >= lax.broadcasted_iota(jnp.int32, (C, C), 1)).astype(F32)
    n_levels = max(1, int(math.log2(C)))
    inv_half = 1.0 / half

    def seg_sum(x):
        s0 = jnp.sum(jnp.where(m0, x, 0.0), axis=-1, keepdims=True)
        s1 = jnp.sum(jnp.where(m0, 0.0, x), axis=-1, keepdims=True)
        return jnp.where(m0, s0, s1)

    def stack(x):
        return jnp.concatenate([jnp.where(m0, x, 0.0), jnp.where(m0, 0.0, x)], axis=0)

    def each(f, *lists):
        return [f(*xs) for xs in zip(*lists)]

    lss = [slice(i * LANE, (i + 1) * LANE) for i in range(n_par)]

    def chunk(c, states):
        sl = pl.ds(pl.multiple_of(c * C, C), C)
        S = list(states)
        r = [r_ref[0, 0, sl, ls] for ls in lss]
        k = [k_ref[0, 0, sl, ls] for ls in lss]
        v = [v_ref[0, 0, sl, ls] for ls in lss]
        a = [a_ref[0, sl, ls] for ls in lss]
        logd = [-jnp.exp(-_softplus(-w_ref[0, sl, ls]) - 0.5) for ls in lss]
        if has_vres:
            v = [vi + (vf_ref[0, sl, ls] - vi) * vg_ref[0, sl, ls] for vi, ls in zip(v, lss)]
        cum = each(lambda x: jnp.dot(tri_c, x, precision=HI, preferred_element_type=F32), logd)
        kkr = [ki * kk_ref[:, ls] for ki, ls in zip(k, lss)]
        kk = each(lambda x: x / jnp.maximum(jnp.sqrt(seg_sum(x * x)), 1e-12), kkr)
        k2 = [ki * (1.0 + (ai - 1.0) * ka_ref[:, ls]) for ki, ai, ls in zip(k, a, lss)]
        eg = each(jnp.exp, cum)
        eng = each(lambda x: jnp.exp(-x), cum)
        a_st = each(lambda kki, ci, li: stack(-kki * jnp.exp(ci - li)), kk, cum, logd)
        r_st = each(lambda ri, e: stack(ri * e), r, eg)
        b_st = each(lambda kki, ai, e: stack(kki * ai * e), kk, a, eng)
        k_st = each(lambda ki, e: stack(ki * e), k2, eng)
        v_st = each(stack, v)
        ar = each(lambda x, y: jnp.concatenate([x, y], axis=0), a_st, r_st)
        xb = each(_dot_nt, ar, b_st)
        xk = each(_dot_nt, ar, k_st)
        p = each(lambda x: jnp.where(stril, x[:2 * C], 0.0), xb)
        lak = each(lambda x: jnp.where(stril, x[:2 * C], 0.0), xk)
        mrb = each(lambda x: jnp.where(tril, x[2 * C:], 0.0), xb)
        mrk = each(lambda x: jnp.where(tril, x[2 * C:], 0.0), xk)
        q = p
        for _ in range(1, n_levels):
            p = each(_dot, p, p)
            q = each(lambda qi, pi: qi + pi + _dot(qi, pi), q, p)
        lv = each(_dot, lak, v_st)
        mv = each(_dot, mrk, v_st)
        a_s = each(_dot_nt, ar, S)
        wm = each(lambda x, y: x[:2 * C] + y, a_s, lv)
        u = each(lambda qi, w: w + _dot(qi, w), q, wm)
        y_st = each(lambda x, m, ui, mvi: x[2 * C:] + _dot(m, ui) + mvi, a_s, mrb, u, mv)
        y = each(lambda x: x[:C] + x[C:], y_st)
        eg_last = each(lambda e: e[C - 1:C, :], eg)
        S_new = each(lambda Si, e, ui, b, vs, ks: Si * e + _dot_tn(ui, b * e) + _dot_tn(vs, ks * e),
                     S, eg_last, u, b_st, v_st, k_st)
        mean = each(lambda x: seg_sum(x) * inv_half, y)
        yc = each(lambda x, m: x - m, y, mean)
        var = each(lambda x: seg_sum(x * x) * inv_half, yc)
        for i, ls in enumerate(lss):
            yn = yc[i] * lax.rsqrt(var[i] + GN_EPS) * lnw_ref[:, ls] + lnb_ref[:, ls]
            bonus = seg_sum(r[i] * k2[i] * rk_ref[:, ls]) * v[i]
            z_ref[0, sl, ls] = ((yn + bonus) * g_ref[0, sl, ls]).astype(z_ref.dtype)
        return tuple(S_new)

    states = lax.fori_loop(0, n_chunks, chunk, tuple(s_scr[i] for i in range(n_par)))
    for i in range(n_par):
        s_scr[i] = states[i]

    @pl.when(t_id == pl.num_programs(2) - 1)
    def _():
        st_ref[0] = s_scr[...]


def _rwkv_scan(rkv, w_pre, a_gate, g, vres, vecs, s0_bd, C):
    _, B, T, D = rkv.shape
    n_par = SCAN_PAIRS if (D // LANE) % SCAN_PAIRS == 0 else 1
    W = n_par * LANE
    nb = D // W
    tb = _div_tile(T, SCAN_ROWS)
    assert tb % C == 0
    seq_spec = pl.BlockSpec((1, tb, W), lambda b, h, t: (b, t, h))
    rkv_spec = lambda i: pl.BlockSpec((1, 1, tb, W), lambda b, h, t: (i, b, t, h))
    vec_spec = pl.BlockSpec((1, W), lambda b, h, t: (0, h))
    st_spec = pl.BlockSpec((1, n_par, LANE, LANE), lambda b, h, t: (b, h, 0, 0))
    has_vres = vres is not None
    n_seq = 6 + (2 if has_vres else 0)
    in_specs = [rkv_spec(0), rkv_spec(1), rkv_spec(2)] + [seq_spec] * (n_seq - 3) + [vec_spec] * 5 + [st_spec]
    args = [rkv, rkv, rkv, w_pre, a_gate, g] + (list(vres) if has_vres else []) + list(vecs) + [s0_bd]
    est = 2 * (n_seq * _nbytes((tb, W), F32) + _nbytes((tb, W), BF16) + 2 * _nbytes((n_par, LANE, LANE), F32))
    est += n_par * 64 * _nbytes((2 * C, LANE), F32)
    return pl.pallas_call(
        functools.partial(_scan_body, C, n_par, tb // C, has_vres),
        grid=(B, nb, T // tb),
        in_specs=in_specs,
        out_specs=[seq_spec, st_spec],
        out_shape=[jax.ShapeDtypeStruct((B, T, D), BF16),
                   jax.ShapeDtypeStruct((B, D // LANE, LANE, LANE), F32)],
        scratch_shapes=[pltpu.VMEM((n_par, LANE, LANE), F32)],
        compiler_params=_params(("parallel", "parallel", "arbitrary"), est),
        name="rwkv_scan",
    )(*args)


def _to_block_diag(s):
    B, H, N, _ = s.shape
    s = s.reshape(B, H // 2, 2, N, N)
    z = jnp.zeros_like(s[:, :, 0])
    top = jnp.concatenate([s[:, :, 0], z], axis=-1)
    bot = jnp.concatenate([z, s[:, :, 1]], axis=-1)
    return jnp.concatenate([top, bot], axis=-2)


def _from_block_diag(s_bd, N):
    B, nb = s_bd.shape[:2]
    return jnp.stack([s_bd[:, :, :N, :N], s_bd[:, :, N:, N:]], axis=2).reshape(B, 2 * nb, N, N)


def _rwkv_step_body(has_vres, mxu_round, *refs):
    if has_vres:
        (r_ref, k_ref, v_ref, w_ref, a_ref, g_ref, vg_ref, vf_ref,
         kk_ref, ka_ref, rk_ref, lnw_ref, lnb_ref, s0_ref, z_ref, st_ref) = refs
    else:
        (r_ref, k_ref, v_ref, w_ref, a_ref, g_ref,
         kk_ref, ka_ref, rk_ref, lnw_ref, lnb_ref, s0_ref, z_ref, st_ref) = refs
    r, k, v, a = r_ref[0], k_ref[0], v_ref[0], a_ref[0]
    S = s0_ref[0]
    N = S.shape[-1]
    eye = (lax.broadcasted_iota(jnp.int32, (N, N), 0) == lax.broadcasted_iota(jnp.int32, (N, N), 1))
    col = lambda x: jnp.sum(jnp.where(eye, x, 0.0), axis=-1, keepdims=True)
    row = lambda x: jnp.sum(jnp.where(eye, x, 0.0), axis=-2, keepdims=True)
    rnd = (lambda x: x.astype(BF16).astype(F32)) if mxu_round else (lambda x: x)
    decay = jnp.exp(-jnp.exp(-_softplus(-w_ref[0]) - 0.5))
    if has_vres:
        v = v + (vf_ref[0] - v) * vg_ref[0]
    kkr = k * kk_ref[...]
    kk = kkr / jnp.maximum(jnp.sqrt(jnp.sum(kkr * kkr, axis=-1, keepdims=True)), 1e-12)
    k2 = k * (1.0 + (a - 1.0) * ka_ref[...])
    sa = jnp.sum(rnd(S) * rnd(-kk), axis=-1, keepdims=True)
    S_new = S * decay + sa * (kk * a) + col(v) * k2
    st_ref[0] = S_new
    y = jnp.sum(rnd(S_new) * rnd(r), axis=-1, keepdims=True)
    mean = jnp.mean(y, axis=-2, keepdims=True)
    yc = y - mean
    var = jnp.mean(yc * yc, axis=-2, keepdims=True)
    yn = row(yc * lax.rsqrt(var + GN_EPS)) * lnw_ref[...] + lnb_ref[...]
    bonus = jnp.sum(r * k2 * rk_ref[...], axis=-1, keepdims=True) * v
    z_ref[0] = (yn + bonus) * g_ref[0]


def _rwkv_step(seqs, vecs, s0, mxu_round):
    Bd, H, N, _ = s0.shape
    row_spec = pl.BlockSpec((1, H, 1, N), lambda b: (b, 0, 0, 0))
    vec_spec = pl.BlockSpec((H, 1, N), lambda b: (0, 0, 0))
    st_spec = pl.BlockSpec((1, H, N, N), lambda b: (b, 0, 0, 0))
    est = 24 * _nbytes((H, N, LANE), F32)
    return pl.pallas_call(
        functools.partial(_rwkv_step_body, len(seqs) == 8, mxu_round),
        grid=(Bd,),
        in_specs=[row_spec] * len(seqs) + [vec_spec] * 5 + [st_spec],
        out_specs=[row_spec, st_spec],
        out_shape=[jax.ShapeDtypeStruct((Bd, H, 1, N), F32), jax.ShapeDtypeStruct((Bd, H, N, N), F32)],
        compiler_params=_params(("parallel",), est),
        name="rwkv_step",
    )(*seqs, *vecs, s0)


def _cumsum_body(tb, lf_ref, f_ref):
    T = lf_ref.shape[1]
    tri = (lax.broadcasted_iota(jnp.int32, (tb, tb), 0)
           >= lax.broadcasted_iota(jnp.int32, (tb, tb), 1)).astype(F32)

    def blk(i, carry):
        sl = pl.ds(pl.multiple_of(i * tb, tb), tb)
        f = jnp.dot(tri, lf_ref[0, sl, :], precision=HI, preferred_element_type=F32) + carry
        f_ref[0, sl, :] = f
        return f[tb - 1:tb, :]

    lax.fori_loop(0, T // tb, blk, jnp.zeros((1, lf_ref.shape[2]), F32))


def _cumsum_rows(lf):
    B, T, W = lf.shape
    tb = _div_tile(T, 256)
    return pl.pallas_call(
        functools.partial(_cumsum_body, tb),
        grid=(B,),
        in_specs=[pl.BlockSpec((1, T, W), lambda b: (b, 0, 0))],
        out_specs=pl.BlockSpec((1, T, W), lambda b: (b, 0, 0)),
        out_shape=jax.ShapeDtypeStruct((B, T, W), F32),
        compiler_params=_params(("parallel",), 4 * _nbytes((T, W), F32)),
        name="logf_cumsum",
    )(lf)


def _fox_prefill_body(tq, scale, q_ref, k_ref, v_ref, fr_ref, fc_ref, o_ref, kb_ref, vb_ref):
    T = q_ref.shape[2]
    kb_ref[...] = k_ref[0, 0].astype(BF16)
    vb_ref[...] = v_ref[0, 0].astype(BF16)
    causal = (lax.broadcasted_iota(jnp.int32, (tq, tq), 1)
              <= lax.broadcasted_iota(jnp.int32, (tq, tq), 0))

    def q_blk(qi, _):
        qs = pl.ds(pl.multiple_of(qi * tq, tq), tq)
        q = q_ref[0, 0, qs, :].astype(BF16)
        fq = fc_ref[0, 0, qs, :]

        def kv_blk(ks, carry, diagonal):
            m, l, acc = carry
            s = _dot_nt(q, kb_ref[ks, :]) * scale + fq - fr_ref[0, 0, :, ks]
            if diagonal:
                s = jnp.where(causal, s, NEG_INF)
            m_new = jnp.maximum(m, jnp.max(s, axis=-1, keepdims=True))
            alpha = jnp.exp(m - m_new)
            p = jnp.exp(s - m_new)
            l = alpha * l + jnp.sum(p, axis=-1, keepdims=True)
            acc = alpha * acc + _dot(p, vb_ref[ks, :])
            return m_new, l, acc

        init = (jnp.full((tq, 1), NEG_INF, F32), jnp.zeros((tq, 1), F32), jnp.zeros((tq, LANE), F32))
        carry = lax.fori_loop(
            0, qi, lambda kj, c: kv_blk(pl.ds(pl.multiple_of(kj * tq, tq), tq), c, False), init)
        _, l, acc = kv_blk(qs, carry, True)
        o_ref[0, qs, :] = (acc / l).astype(o_ref.dtype)
        return 0

    lax.fori_loop(0, T // tq, q_blk, 0)


def _fox_prefill(qk, v, f_row, f_col, head_dim):
    _, B, T, D = qk.shape
    H = D // head_dim
    assert head_dim == LANE
    tq = _div_tile(T, 512)
    spec = lambda i: pl.BlockSpec((1, 1, T, LANE), lambda b, h: (i, b, 0, h))
    est = 2 * (3 * _nbytes((T, LANE), F32) + _nbytes((T, LANE), F32) + 2 * _nbytes((T, LANE), BF16))
    est += 6 * _nbytes((tq, tq), F32)
    return pl.pallas_call(
        functools.partial(_fox_prefill_body, tq, head_dim ** -0.5),
        grid=(B, H),
        in_specs=[spec(0), spec(1), spec(0),
                  pl.BlockSpec((1, 1, 1, T), lambda b, h: (b, h, 0, 0)),
                  pl.BlockSpec((1, 1, T, 1), lambda b, h: (b, h, 0, 0))],
        out_specs=pl.BlockSpec((1, T, LANE), lambda b, h: (b, 0, h)),
        out_shape=jax.ShapeDtypeStruct((B, T, D), BF16),
        scratch_shapes=[pltpu.VMEM((T, LANE), BF16), pltpu.VMEM((T, LANE), BF16)],
        compiler_params=_params(("parallel", "parallel"), est),
        name="fox_prefill",
    )(qk, qk, v, f_row, f_col)


def _fox_decode_body(n_heads, scale, pt_ref, q_ref, kn_ref, vn_ref, lfn_ref, kp_ref, vp_ref, lfp_ref,
                     o_ref, qm_ref, m_ref, l_ref, acc_ref, carry_ref):
    p_id = pl.program_id(1)
    n_pages = pl.num_programs(1)
    P = kp_ref.shape[0] // n_heads
    n_grp = n_heads // SUBLANE
    head_rows = lambda ref, h: ref[pl.ds(h, P, stride=n_heads), :]
    sub = lax.broadcasted_iota(jnp.int32, (SUBLANE, LANE), 0)

    @pl.when(p_id == 0)
    def _():
        q = q_ref[0]
        for h in range(n_heads):
            qm_ref[h] = jnp.where(sub == h % SUBLANE, q[h:h + 1, :], 0.0)
        m_ref[...] = jnp.full(m_ref.shape, NEG_INF, F32)
        l_ref[...] = jnp.zeros(l_ref.shape, F32)
        acc_ref[...] = jnp.zeros(acc_ref.shape, F32)
        carry_ref[...] = lfn_ref[0]

    lf = lfp_ref[...]
    pos_r = lax.broadcasted_iota(jnp.int32, (P, 2 * P), 0)
    pos_c = lax.broadcasted_iota(jnp.int32, (P, 2 * P), 1)
    sums = lax.dot_general(lf, ((pos_r > pos_c) | (pos_c >= P)).astype(F32), (((0,), (0,)), ((), ())),
                           precision=HI, preferred_element_type=F32)
    bias = sums[:, :P] + carry_ref[...]
    carry_ref[...] = carry_ref[...] + sums[:, P:P + 1]

    s_grp = []
    for gi in range(n_grp):
        s = jnp.zeros((SUBLANE, P), F32)
        for hh in range(SUBLANE):
            h = gi * SUBLANE + hh
            s = s + _dot_nt(qm_ref[h], head_rows(kp_ref, h))
        s_grp.append(s)
    s = jnp.concatenate(s_grp, axis=0) * scale + bias
    m_new = jnp.maximum(m_ref[...], jnp.max(s, axis=-1, keepdims=True))
    alpha = jnp.exp(m_ref[...] - m_new)
    p = jnp.exp(s - m_new)
    l_ref[...] = alpha * l_ref[...] + jnp.sum(p, axis=-1, keepdims=True)
    m_ref[...] = m_new
    pv_grp = []
    for gi in range(n_grp):
        pg = p[gi * SUBLANE:(gi + 1) * SUBLANE, :]
        pv = jnp.zeros((SUBLANE, LANE), F32)
        for hh in range(SUBLANE):
            h = gi * SUBLANE + hh
            pv = pv + _dot(jnp.where(sub == hh, pg, 0.0), head_rows(vp_ref, h))
        pv_grp.append(pv)
    acc_ref[...] = alpha * acc_ref[...] + jnp.concatenate(pv_grp, axis=0)

    @pl.when(p_id == n_pages - 1)
    def _():
        s_new = jnp.sum(q_ref[0] * kn_ref[0], axis=-1, keepdims=True) * scale
        m_fin = jnp.maximum(m_ref[...], s_new)
        al = jnp.exp(m_ref[...] - m_fin)
        p_new = jnp.exp(s_new - m_fin)
        l_fin = al * l_ref[...] + p_new
        o_ref[0] = ((al * acc_ref[...] + p_new * vn_ref[0]) / l_fin).astype(o_ref.dtype)


def _fox_decode(q, k_new, v_new, lf_new, cache_k, cache_v, cache_lft, layer, page_table, head_dim):
    Bd, H, hd = q.shape
    assert hd == LANE and H % SUBLANE == 0
    P = cache_lft.shape[2]
    n_pages = page_table.shape[1]
    row = lambda: pl.BlockSpec((1, H, hd), lambda b, p, pt: (b, 0, 0))
    page_idx = lambda b, p, pt: (layer, pt[b, n_pages - 1 - p], 0, 0)
    gs = pltpu.PrefetchScalarGridSpec(
        num_scalar_prefetch=1,
        grid=(Bd, n_pages),
        in_specs=[row(), row(), row(),
                  pl.BlockSpec((1, H, 1), lambda b, p, pt: (b, 0, 0)),
                  pl.BlockSpec((None, None, P * H, hd), page_idx),
                  pl.BlockSpec((None, None, P * H, hd), page_idx),
                  pl.BlockSpec((None, None, P, H), page_idx)],
        out_specs=pl.BlockSpec((1, H, hd), lambda b, p, pt: (b, 0, 0)),
        scratch_shapes=[pltpu.VMEM((H, SUBLANE, LANE), F32),
                        pltpu.VMEM((H, 1), F32), pltpu.VMEM((H, 1), F32),
                        pltpu.VMEM((H, LANE), F32), pltpu.VMEM((H, 1), F32)])
    est = 4 * _nbytes((P * H, hd), F32) + 2 * MIB
    return pl.pallas_call(
        functools.partial(_fox_decode_body, H, head_dim ** -0.5),
        grid_spec=gs,
        out_shape=jax.ShapeDtypeStruct((Bd, H, hd), BF16),
        compiler_params=_params(("parallel", "arbitrary"), est),
        name="fox_decode",
    )(page_table, q, k_new, v_new, lf_new, cache_k, cache_v, cache_lft)


def _head_rms(acc, g_row, head_dim):
    tm, tn = acc.shape
    parts = []
    for h in range(tn // head_dim):
        blk = acc[:, h * head_dim:(h + 1) * head_dim]
        blk = blk * lax.rsqrt(jnp.mean(blk * blk, axis=-1, keepdims=True) + NORM_EPS)
        parts.append(blk * g_row)
    return parts[0] if len(parts) == 1 else jnp.concatenate(parts, axis=-1)


def _qkv_proj(h, w_qkvf, layer, qk_gain, head_dim):
    _, M, D = h.shape
    tm = _row_tile(M)
    tn = _div_tile(D, 512 if M > 64 else 1024)
    nj = D // tn

    def epi(acc, gq):
        return _head_rms(acc, gq, head_dim)

    (qk,) = _linear(
        (M // tm, 2 * nj), h,
        pl.BlockSpec((1, tm, D), lambda i, j: (0, i, 0)),
        [w_qkvf], pl.BlockSpec((1, D, tn), lambda i, j: (layer, 0, j)),
        [(qk_gain, pl.BlockSpec((1, 1, head_dim), lambda i, j: (j // nj, 0, 0)))],
        [(jax.ShapeDtypeStruct((2, M, D), F32),
          pl.BlockSpec((1, tm, tn), lambda i, j: (j // nj, i, j % nj)))],
        epi, ("parallel", "arbitrary"), name="qk_proj")
    (v,) = _linear(
        (M // tm, nj), h,
        pl.BlockSpec((1, tm, D), lambda i, j: (0, i, 0)),
        [w_qkvf], pl.BlockSpec((1, D, tn), lambda i, j: (layer, 0, 2 * nj + j)),
        [],
        [(jax.ShapeDtypeStruct((1, M, D), F32), pl.BlockSpec((1, tm, tn), lambda i, j: (0, i, j)))],
        lambda acc: acc, ("parallel", "arbitrary"), name="v_proj")
    return qk, v


def _swiglu_epi(g, u):
    return (g * jax.nn.sigmoid(g)) * u


def _col_segments(F, tn):
    assert F % LANE == 0 and tn % LANE == 0
    n = F // tn
    segs = [(0, tn, n)] if n else []
    if F - n * tn:
        segs.append((n * tn, F - n * tn, 1))
    return segs


def _ffn_up_dense(h, w_gate, w_up, layer):
    _, M, D = h.shape
    F = w_gate.shape[2]
    tm = _row_tile(M)
    E = pl.Element
    segs = _col_segments(F, 256 if M > 64 else 512)
    out = jnp.zeros((M, F), BF16) if len(segs) > 1 else None
    for c0, tn, n_j in segs:
        off = lambda j: pl.multiple_of(c0 + j * tn, LANE)
        (out,) = _linear(
            (M // tm, n_j), h,
            pl.BlockSpec((1, tm, D), lambda i, j: (0, i, 0)),
            [w_gate, w_up], pl.BlockSpec((None, E(D), E(tn)), lambda i, j: (layer, 0, off(j))),
            [],
            [(jax.ShapeDtypeStruct((M, F), BF16),
              pl.BlockSpec((E(tm), E(tn)), lambda i, j: (i * tm, off(j))))],
            _swiglu_epi, ("parallel", "arbitrary"), into=out, name="ffn_up")
    return out


def _moe_up_sorted(a_sorted, w_gate, w_up, layer_base, n_exp, tile_expert, tm):
    _, Mp, D = a_sorted.shape
    F = w_gate.shape[2]
    E = pl.Element
    segs = _col_segments(F, 512)
    out = jnp.zeros((Mp, F), BF16) if len(segs) > 1 else None
    for c0, tn, n_j in segs:
        off = lambda j: pl.multiple_of(c0 + j * tn, LANE)
        (out,) = _linear(
            (n_j, Mp // tm), a_sorted,
            pl.BlockSpec((1, tm, D), lambda j, s, te: (0, s, 0)),
            [w_gate, w_up],
            pl.BlockSpec((None, E(D), E(tn)), lambda j, s, te: (layer_base + te[s] % n_exp, 0, off(j))),
            [],
            [(jax.ShapeDtypeStruct((Mp, F), BF16),
              pl.BlockSpec((E(tm), E(tn)), lambda j, s, te: (s * tm, off(j))))],
            _swiglu_epi, ("parallel", "arbitrary"), prefetch=tile_expert, into=out,
            is_padding=lambda te: te[pl.program_id(1)] >= n_exp, name="moe_up")
    return out


def _moe_down_sorted(a_sorted, w_down, layer_base, n_exp, tile_expert, pair_gate, tm):
    _, Mp, F = a_sorted.shape
    D = w_down.shape[2]
    tn = _div_tile(D, 512)
    (out,) = _linear(
        (D // tn, Mp // tm), a_sorted,
        pl.BlockSpec((1, tm, F), lambda j, s, te: (0, s, 0)),
        [w_down], pl.BlockSpec((1, F, tn), lambda j, s, te: (layer_base + te[s] % n_exp, 0, j)),
        [(pair_gate, pl.BlockSpec((tm, 1), lambda j, s, te: (s, 0)))],
        [(jax.ShapeDtypeStruct((Mp, D), F32), pl.BlockSpec((tm, tn), lambda j, s, te: (s, j)))],
        lambda acc, pg: acc * pg, ("parallel", "arbitrary"), prefetch=tile_expert,
        is_padding=lambda te: te[pl.program_id(1)] >= n_exp, name="moe_down")
    return out


def _row_copy(src_ref, row, dst_ref, sem):
    return pltpu.make_async_copy(src_ref.at[pl.ds(row, 1), :], dst_ref, sem)


def _gather_rows_body(n_rows, idx_ref, src_ref, out_ref, buf_ref, sem_ref):
    s = pl.program_id(0)
    slot = s % 2

    def issue(step, to_slot):
        def one(r, carry):
            _row_copy(src_ref, idx_ref[step * n_rows + r], buf_ref.at[to_slot, pl.ds(r, 1), :],
                      sem_ref.at[to_slot]).start()
            return carry
        lax.fori_loop(0, n_rows, one, 0)

    @pl.when(s == 0)
    def _():
        issue(0, 0)

    @pl.when(s + 1 < pl.num_programs(0))
    def _():
        issue(s + 1, 1 - slot)

    def wait_one(r, carry):
        _row_copy(src_ref, 0, buf_ref.at[slot, pl.ds(r, 1), :], sem_ref.at[slot]).wait()
        return carry
    lax.fori_loop(0, n_rows, wait_one, 0)
    out_ref[...] = buf_ref[slot].astype(out_ref.dtype)


def _gather_rows(src, idx, n_rows=256):
    M, D = src.shape
    n_out = idx.shape[0]
    assert n_out % n_rows == 0
    gs = pltpu.PrefetchScalarGridSpec(
        num_scalar_prefetch=1, grid=(n_out // n_rows,),
        in_specs=[pl.BlockSpec(memory_space=pl.ANY)],
        out_specs=pl.BlockSpec((n_rows, D), lambda s, ix: (s, 0)),
        scratch_shapes=[pltpu.VMEM((2, n_rows, D), F32), pltpu.SemaphoreType.DMA((2,))])
    est = 2 * _nbytes((n_rows, D), F32) + 3 * _nbytes((n_rows, D), BF16)
    return pl.pallas_call(
        functools.partial(_gather_rows_body, n_rows),
        grid_spec=gs,
        out_shape=jax.ShapeDtypeStruct((n_out, D), BF16),
        compiler_params=_params(("arbitrary",), est),
        name="moe_gather",
    )(idx, src)


def _combine_body(n_tok, pos_ref, y_ref, x_ref, gate_ref, out_ref, buf_ref, sem_ref):
    n_inner = pl.num_programs(1)
    s = pl.program_id(0) * n_inner + pl.program_id(1)
    n_steps = pl.num_programs(0) * n_inner
    slot = s % 2

    def issue(step, to_slot):
        def one(t, carry):
            for k in range(TOP_K):
                _row_copy(y_ref, pos_ref[(step * n_tok + t) * TOP_K + k],
                          buf_ref.at[to_slot, k, pl.ds(t, 1), :], sem_ref.at[to_slot]).start()
            return carry
        lax.fori_loop(0, n_tok, one, 0)

    @pl.when(s == 0)
    def _():
        issue(0, 0)

    @pl.when(s + 1 < n_steps)
    def _():
        issue(s + 1, 1 - slot)

    def wait_one(t, carry):
        for k in range(TOP_K):
            _row_copy(y_ref, 0, buf_ref.at[slot, k, pl.ds(t, 1), :], sem_ref.at[slot]).wait()
        return carry
    lax.fori_loop(0, n_tok, wait_one, 0)
    acc = buf_ref[slot, 0]
    for k in range(1, TOP_K):
        acc = acc + buf_ref[slot, k]
    out_ref[0] = x_ref[0] + gate_ref[0] * acc


def _moe_combine(y_sorted, pos, x, gate, n_tok=128):
    Bx, R, D = x.shape
    n_tok = _div_tile(R, n_tok)
    gs = pltpu.PrefetchScalarGridSpec(
        num_scalar_prefetch=1, grid=(Bx, R // n_tok),
        in_specs=[pl.BlockSpec(memory_space=pl.ANY),
                  pl.BlockSpec((1, n_tok, D), lambda b, s, ps: (b, s, 0)),
                  pl.BlockSpec((1, 1, D), lambda b, s, ps: (b, 0, 0))],
        out_specs=pl.BlockSpec((1, n_tok, D), lambda b, s, ps: (b, s, 0)),
        scratch_shapes=[pltpu.VMEM((2, TOP_K, n_tok, D), F32), pltpu.SemaphoreType.DMA((2,))])
    est = (2 * TOP_K + 6) * _nbytes((n_tok, D), F32)
    return pl.pallas_call(
        functools.partial(_combine_body, n_tok),
        grid_spec=gs,
        out_shape=jax.ShapeDtypeStruct((Bx, R, D), F32),
        compiler_params=_params(("arbitrary", "arbitrary"), est),
        name="moe_combine",
    )(pos, y_sorted, x, gate)


def _moe_up_all(h, w_gate, w_up, layer_base, n_exp):
    _, R, D = h.shape
    F = w_gate.shape[2]
    E = pl.Element
    segs = _col_segments(F, 512)
    out = jnp.zeros((n_exp, R, F), BF16) if len(segs) > 1 else None
    for c0, tn, n_j in segs:
        off = lambda j: pl.multiple_of(c0 + j * tn, LANE)
        (out,) = _linear(
            (n_exp, n_j), h,
            pl.BlockSpec((1, R, D), lambda e, j: (0, 0, 0)),
            [w_gate, w_up], pl.BlockSpec((None, E(D), E(tn)), lambda e, j: (layer_base + e, 0, off(j))),
            [],
            [(jax.ShapeDtypeStruct((n_exp, R, F), BF16),
              pl.BlockSpec((None, E(R), E(tn)), lambda e, j: (e, 0, off(j))))],
            _swiglu_epi, ("parallel", "arbitrary"), into=out, name="moe_up_all")
    return out


def _moe_down_all_body(a_ref, w_ref, ge_ref, x_ref, gate_ref, o_ref, acc_ref):
    e = pl.program_id(1)

    @pl.when(e == 0)
    def _():
        acc_ref[...] = jnp.zeros(acc_ref.shape, F32)

    acc_ref[...] += ge_ref[0] * _dot(a_ref[0], w_ref[0])

    @pl.when(e == pl.num_programs(1) - 1)
    def _():
        o_ref[0] = x_ref[0] + gate_ref[0] * acc_ref[...]


def _moe_down_all(a, w_down, layer_base, gates_t, x, gate):
    n_exp, R, F = a.shape
    D = w_down.shape[2]
    tn = _div_tile(D, 512)
    est = 2 * (_nbytes((F, tn), F32) + _nbytes((R, F), BF16)) + _nbytes((F, tn), BF16) + 8 * _nbytes((R, tn), F32)
    return pl.pallas_call(
        _moe_down_all_body,
        grid=(D // tn, n_exp),
        in_specs=[pl.BlockSpec((1, R, F), lambda j, e: (e, 0, 0)),
                  pl.BlockSpec((1, F, tn), lambda j, e: (layer_base + e, 0, j)),
                  pl.BlockSpec((1, R, 1), lambda j, e: (e, 0, 0)),
                  pl.BlockSpec((1, R, tn), lambda j, e: (0, 0, j)),
                  pl.BlockSpec((1, R, tn), lambda j, e: (0, 0, j))],
        out_specs=pl.BlockSpec((1, R, tn), lambda j, e: (0, 0, j)),
        out_shape=jax.ShapeDtypeStruct((1, R, D), F32),
        scratch_shapes=[pltpu.VMEM((R, tn), F32)],
        compiler_params=_params(("parallel", "arbitrary"), est),
        name="moe_down_all",
    )(a, w_down, gates_t, x, gate)


def _route_metadata(gates, selmask, n_exp, tm):
    M = gates.shape[0]
    n_tiles = (M * TOP_K) // tm + n_exp
    sel = selmask > 0.5
    seli = sel.astype(jnp.int32)
    rank = jnp.cumsum(seli, axis=0) - seli
    counts = jnp.sum(seli, axis=0)
    tiles_per = (counts + tm - 1) // tm
    tile_start = jnp.cumsum(tiles_per) - tiles_per
    pos = tile_start[None, :] * tm + rank
    tile_ids = jnp.arange(n_tiles, dtype=jnp.int32)
    tile_expert = jnp.clip(jnp.searchsorted(jnp.cumsum(tiles_per), tile_ids, side="right"),
                           0, n_exp - 1).astype(jnp.int32)
    tile_expert = jnp.where(tile_ids < jnp.sum(tiles_per), tile_expert, tile_expert + n_exp)
    flat_pos = jnp.where(sel, pos, n_tiles * tm).reshape(-1)
    pair = jnp.zeros((n_tiles * tm,), jnp.int32).at[flat_pos].set(
        jnp.arange(1, M * n_exp + 1, dtype=jnp.int32), mode="drop")
    pair_idx = jnp.maximum(pair - 1, 0)
    row_src = pair_idx // n_exp
    row_gate = jnp.where(pair > 0, gates.reshape(-1)[pair_idx], 0.0)
    order = jnp.argsort(jnp.where(sel, 0, 1), axis=1, stable=True)[:, :TOP_K]
    tok_pos = jnp.take_along_axis(pos, order, axis=1).astype(jnp.int32).reshape(-1)
    return row_src, tile_expert, row_gate.reshape(-1, 1), tok_pos


def _trunk(x, mods, rows_per_batch, seq, P, shift_in, wkv_in, paged):
    Bx, R, D = x.shape
    M = Bx * R
    depth = P["norm_g"].shape[0]
    n_rwkv_heads, n_state = P["state_dims"]
    head_dim = P["attn_q_g"].shape[-1]
    n_attn_heads = D // head_dim
    n_exp = P["moe_w_router"].shape[-1]
    lo_w, lo_a, lo_g = P["rwkv_w1"].shape[-1], P["rwkv_a1"].shape[-1], P["rwkv_g1"].shape[-1]
    lo_v = P["rwkv_v1"].shape[-1]
    n_seqs = Bx if seq else R
    T = R if seq else 1
    flat = lambda t: t.reshape(1, M, t.shape[-1])

    shifts, wkvs, ks, vs, lfs = [], [], [], [], []
    v_first = None
    for i in range(depth):
        j = i // 2
        shift, scale, gate = mods[(i, 0)]
        g_row = P["norm_g"][i, 0][None, :]
        if i % 2 == 0:
            mu = P["rwkv_mu"][j][jnp.array([0, 2, 3, 1, 4, 5])]
            if seq:
                sp = jnp.zeros((Bx, 1, D), F32) if shift_in is None else shift_in[j][:, None, :]
            else:
                sp = shift_in[j][None]
            lerps, h_keep = _prep_rwkv(x, g_row, shift, scale, mu, sp, seq)
            lerps = lerps.reshape(N_LERP, M, D)
            shifts.append(h_keep.reshape(n_seqs, D))
            n_l = P["rwkv_w_rkv"].shape[0]
            w_rkv = P["rwkv_w_rkv"].reshape(n_l * 3, D, D)
            tm = _row_tile(M)
            tn = _div_tile(D, 512 if M > 64 else 1024)
            (rkv,) = _linear(
                (3, M // tm, D // tn), lerps,
                pl.BlockSpec((1, tm, D), lambda s, a, b: (s, a, 0)),
                [w_rkv], pl.BlockSpec((1, D, tn), lambda s, a, b: (3 * j + s, 0, b)),
                [],
                [(jax.ShapeDtypeStruct((3, M, D), F32), pl.BlockSpec((1, tm, tn), lambda s, a, b: (s, a, b)))],
                lambda acc: acc, ("parallel", "parallel", "arbitrary"), name="rkv_proj")
            row = lambda name: P[name][j][None, :]
            w_mid = _proj(lerps, P["rwkv_w1"], j, a_lead=3, epilogue=jnp.tanh, out_dtype=BF16, name="lora_w1")
            w_pre = _proj(w_mid[None], P["rwkv_w2"], j, extras=[row("rwkv_w0")],
                          epilogue=lambda acc, b: acc + b, name="lora_w2")
            a_mid = _proj(lerps, P["rwkv_a1"], j, a_lead=4, out_dtype=BF16, name="lora_a1")
            a_gate = _proj(a_mid[None], P["rwkv_a2"], j, extras=[row("rwkv_a0")],
                           epilogue=lambda acc, b: jax.nn.sigmoid(acc + b), name="lora_a2")
            g_mid = _proj(lerps, P["rwkv_g1"], j, a_lead=5, epilogue=jax.nn.sigmoid, out_dtype=BF16,
                          name="lora_g1")
            g_out = _proj(g_mid[None], P["rwkv_g2"], j, name="lora_g2")
            if j == 0:
                vres = None
                v_first = rkv[2]
            else:
                v_mid = _proj(lerps, P["rwkv_v1"], j - 1, a_lead=2, out_dtype=BF16, name="lora_v1")
                v_gate = _proj(v_mid[None], P["rwkv_v2"], j - 1, extras=[P["rwkv_v0"][j - 1][None, :]],
                               epilogue=lambda acc, b: jax.nn.sigmoid(acc + b), name="lora_v2")
                vres = (v_gate, v_first)
            s0 = (jnp.zeros((n_seqs, n_rwkv_heads, n_state, n_state), F32) if wkv_in is None else wkv_in[j])
            vecs = (row("rwkv_k_k"), row("rwkv_k_a"), P["rwkv_r_k"][j].reshape(1, D),
                    row("rwkv_ln_w"), row("rwkv_ln_b"))
            if seq:
                shp = lambda t: t.reshape(n_seqs, T, D)
                z, s_bd = _rwkv_scan(rkv.reshape(3, n_seqs, T, D), shp(w_pre), shp(a_gate), shp(g_out),
                                     None if vres is None else tuple(shp(t) for t in vres),
                                     vecs, _to_block_diag(s0), SCAN_CHUNK)
                wkvs.append(_from_block_diag(s_bd, n_state))
                z = z.reshape(1, M, D)
            else:
                hs = lambda t: t.reshape(n_seqs, n_rwkv_heads, 1, n_state)
                seqs = [hs(rkv[0]), hs(rkv[1]), hs(rkv[2]), hs(w_pre), hs(a_gate), hs(g_out)]
                if vres is not None:
                    seqs += [hs(vres[0]), hs(vres[1])]
                z, s_new = _rwkv_step(seqs, [t.reshape(n_rwkv_heads, 1, n_state) for t in vecs], s0,
                                      STEP_MXU_ROUND)
                wkvs.append(s_new)
                z = z.reshape(1, M, D)
            x = _proj_residual(z, P["rwkv_w_o"], j, x.reshape(M, D), gate, rows_per_batch,
                               name="rwkv_out").reshape(Bx, R, D)
        else:
            h = _prep_plain(x, g_row, shift, scale)
            qk, v_att = _qkv_proj(flat(h), P["attn_w_qkvf"], j,
                                  jnp.stack([P["attn_q_g"][j], P["attn_k_g"][j]])[:, None, :], head_dim)
            w_f = jnp.pad(P["attn_w_qkvf"][j][:, 3 * D:], ((0, 0), (0, LANE - n_attn_heads)))[None]
            b_f = jnp.pad(P["attn_b_f"][j], (0, LANE - n_attn_heads))[None, :]
            logf = _proj(flat(h), w_f, 0, extras=[b_f], epilogue=lambda acc, b: -_softplus(-(acc + b)),
                         name="logf_proj")
            ks.append(qk[1].reshape(n_seqs, T, n_attn_heads, head_dim))
            vs.append(v_att[0].reshape(n_seqs, T, n_attn_heads, head_dim))
            lfs.append(logf[:, :n_attn_heads].reshape(n_seqs, T, n_attn_heads))
            if not paged:
                f = _cumsum_rows(logf.reshape(n_seqs, T, LANE))[:, :, :n_attn_heads]
                f_bht = jnp.transpose(f, (0, 2, 1))
                o = _fox_prefill(qk.reshape(2, n_seqs, T, D), v_att.reshape(1, n_seqs, T, D),
                                 f_bht[:, :, None, :], f_bht[:, :, :, None], head_dim)
            else:
                hv = lambda t: t.reshape(n_seqs, n_attn_heads, head_dim)
                o = _fox_decode(hv(qk[0]), hv(qk[1]), hv(v_att[0]),
                                logf[:, :n_attn_heads].reshape(n_seqs, n_attn_heads, 1),
                                P["cache_k"], P["cache_v"], P["cache_lft"], j, P["page_table"], head_dim)
            x = _proj_residual(o.reshape(1, M, D), P["attn_w_o"], j, x.reshape(M, D), gate, rows_per_batch,
                               name="attn_out").reshape(Bx, R, D)

        shift, scale, gate = mods[(i, 1)]
        g_row = P["norm_g"][i, 1][None, :]
        if i % 2 == 0:
            h = _prep_plain(x, g_row, shift, scale)
            hid = _ffn_up_dense(flat(h), P["ffn_w_gate"], P["ffn_w_up"], j)
            F = hid.shape[1]
            part = _proj_residual(hid[None], P["ffn_w_down"], j, None, None, rows_per_batch,
                                  k_blk=0, k_size=F // 2, final=False, name="ffn_down0")
            x = _proj_residual(hid[None], P["ffn_w_down"], j, x.reshape(M, D), gate, rows_per_batch,
                               k_blk=1, k_size=F // 2, prev=part, name="ffn_down1").reshape(Bx, R, D)
        else:
            w_router = jnp.pad(P["moe_w_router"][j], ((0, 0), (0, LANE - n_exp)))
            n_l = P["moe_w_gate"].shape[0]
            wg = P["moe_w_gate"].reshape((n_l * n_exp,) + P["moe_w_gate"].shape[2:])
            wu = P["moe_w_up"].reshape((n_l * n_exp,) + P["moe_w_up"].shape[2:])
            wd = P["moe_w_down"].reshape((n_l * n_exp,) + P["moe_w_down"].shape[2:])
            if seq:
                h32, gates, selmask = _prep_moe(x, g_row, shift, scale, w_router, n_exp, F32)
                tm = _div_tile(M, 512)
                row_src, tile_expert, row_gate, tok_pos = _route_metadata(
                    gates.reshape(M, LANE)[:, :n_exp], selmask.reshape(M, LANE)[:, :n_exp], n_exp, tm)
                a_sorted = _gather_rows(h32.reshape(M, D), row_src)
                hid = _moe_up_sorted(a_sorted[None], wg, wu, j * n_exp, n_exp, tile_expert, tm)
                y = _moe_down_sorted(hid[None], wd, j * n_exp, n_exp, tile_expert, row_gate, tm)
                x = _moe_combine(y, tok_pos, x, gate)
            else:
                hb, gates, _ = _prep_moe(x, g_row, shift, scale, w_router, n_exp, BF16)
                hid = _moe_up_all(hb, wg, wu, j * n_exp, n_exp)
                gates_t = jnp.transpose(gates.reshape(M, LANE)[:, :n_exp])[:, :, None]
                x = _moe_down_all(hid, wd, j * n_exp, gates_t, x, gate)
    return x, ks, vs, lfs, shifts, wkvs


def kernel(x_prompt, x_sample, c_prompt, c_sample, cache_k, cache_v, cache_logf, page_table, state_shift, state_wkv, ada_w, ada_b, norm_g, rwkv_mu, rwkv_w_rkv, rwkv_w0, rwkv_w1, rwkv_w2, rwkv_a0, rwkv_a1, rwkv_a2, rwkv_v0, rwkv_v1, rwkv_v2, rwkv_g1, rwkv_g2, rwkv_k_k, rwkv_k_a, rwkv_r_k, rwkv_ln_w, rwkv_ln_b, rwkv_w_o, attn_w_qkvf, attn_b_f, attn_q_g, attn_k_g, attn_w_o, ffn_w_gate, ffn_w_up, ffn_w_down, moe_w_router, moe_w_gate, moe_w_up, moe_w_down):
    B, T, D = x_prompt.shape
    Bd = x_sample.shape[0]
    depth = ada_w.shape[0]
    n_layers_attn, n_pool, page, n_attn_heads, head_dim = cache_k.shape

    P = dict(
        norm_g=norm_g, rwkv_mu=rwkv_mu, rwkv_w_rkv=rwkv_w_rkv, rwkv_w0=rwkv_w0, rwkv_w1=rwkv_w1,
        rwkv_w2=rwkv_w2, rwkv_a0=rwkv_a0, rwkv_a1=rwkv_a1, rwkv_a2=rwkv_a2, rwkv_v0=rwkv_v0,
        rwkv_v1=rwkv_v1, rwkv_v2=rwkv_v2, rwkv_g1=rwkv_g1, rwkv_g2=rwkv_g2, rwkv_k_k=rwkv_k_k,
        rwkv_k_a=rwkv_k_a, rwkv_r_k=rwkv_r_k, rwkv_ln_w=rwkv_ln_w, rwkv_ln_b=rwkv_ln_b,
        rwkv_w_o=rwkv_w_o, attn_w_qkvf=attn_w_qkvf, attn_b_f=attn_b_f, attn_q_g=attn_q_g,
        attn_k_g=attn_k_g, attn_w_o=attn_w_o, ffn_w_gate=ffn_w_gate, ffn_w_up=ffn_w_up,
        ffn_w_down=ffn_w_down, moe_w_router=moe_w_router, moe_w_gate=moe_w_gate, moe_w_up=moe_w_up,
        moe_w_down=moe_w_down,
        state_dims=(state_wkv.shape[2], state_wkv.shape[3]),
        cache_k=cache_k.reshape(n_layers_attn, n_pool, page * n_attn_heads, head_dim),
        cache_v=cache_v.reshape(n_layers_attn, n_pool, page * n_attn_heads, head_dim),
        cache_lft=cache_logf,
        page_table=page_table,
    )

    n_rows = B + Bd
    rows_pad = -(-n_rows // SUBLANE) * SUBLANE
    c_all = jnp.pad(jnp.concatenate([c_prompt, c_sample], axis=0), ((0, rows_pad - n_rows), (0, 0)))
    mods_all = _ada_mods(c_all, ada_w.reshape(depth * 2, D, 3 * D), ada_b.reshape(depth * 2, 1, 3 * D))

    def split_mods(r0, r1, per_row):
        out = {}
        for i in range(depth):
            for s in range(2):
                m = mods_all[i * 2 + s, r0:r1]
                parts = [m[:, k * D:(k + 1) * D] for k in range(3)]
                out[(i, s)] = tuple(p[None] if per_row else p[:, None, :] for p in parts)
        return out

    yp, ks, vs, lfs, shifts, wkvs = _trunk(
        x_prompt, split_mods(0, B, False), T, True, P, None, None, False)
    ys, ks2, vs2, lfs2, shifts2, wkvs2 = _trunk(
        x_sample.reshape(1, Bd, D), split_mods(B, B + Bd, True), 1, False, P, state_shift, state_wkv, True)
    return (yp, ys.reshape(Bd, 1, D),
            jnp.stack(ks), jnp.stack(vs), jnp.stack(lfs),
            jnp.stack(ks2), jnp.stack(vs2), jnp.stack(lfs2),
            jnp.stack(shifts), jnp.stack(wkvs), jnp.stack(shifts2), jnp.stack(wkvs2))
```

```python
import functools
import math

import jax
import jax.numpy as jnp
from jax import lax
from jax.experimental import pallas as pl
from jax.experimental.pallas import tpu as pltpu

F32 = jnp.float32
BF16 = jnp.bfloat16

NORM_EPS = 1e-6
GN_EPS = 64e-5
NEG_INF = -1e30
TOP_K = 2
N_LERP = 6

LANE = 128
SUBLANE = 8
VMEM_BYTES = 64 * 1024 * 1024
VMEM_CAP = VMEM_BYTES - 6 * 1024 * 1024
MIB = 1024 * 1024

HI = lax.Precision.HIGHEST

SCAN_CHUNK = 64
SCAN_PAIRS = 16
SCAN_ROWS = 256
PREFILL_HEADS = 4
STEP_MXU_ROUND = True


def _params(sem, est_bytes):
    limit = min(max(int(est_bytes) + 4 * MIB, 32 * MIB), VMEM_CAP)
    return pltpu.CompilerParams(dimension_semantics=sem, vmem_limit_bytes=limit)


def _nbytes(shape, dtype):
    return math.prod(shape) * jnp.dtype(dtype).itemsize


def _div_tile(n, pref):
    if n <= pref:
        return n
    t = pref
    while n % t:
        t //= 2
    return t


def _dot(a, b):
    return jnp.dot(a.astype(BF16), b.astype(BF16), preferred_element_type=F32)


def _dot_nt(a, b):
    return lax.dot_general(a.astype(BF16), b.astype(BF16), (((1,), (1,)), ((), ())),
                           preferred_element_type=F32)


def _dot_tn(a, b):
    return lax.dot_general(a.astype(BF16), b.astype(BF16), (((0,), (0,)), ((), ())),
                           preferred_element_type=F32)


def _softplus(z):
    return jnp.maximum(z, 0.0) + jnp.log1p(jnp.exp(-jnp.abs(z)))


def _rms_mod(x, g, shift, scale):
    y = x * lax.rsqrt(jnp.mean(x * x, axis=-1, keepdims=True) + NORM_EPS)
    return (y * g) * (1.0 + scale) + shift


def _ada_body(c_ref, w_ref, b_ref, o_ref):
    c = c_ref[...]
    a = c * jax.nn.sigmoid(c)
    o_ref[0] = _dot(a, w_ref[0]) + b_ref[0]


def _ada_mods(c_all, ada_w, ada_b):
    S, D, N = ada_w.shape
    Rp = c_all.shape[0]
    tn = _div_tile(N, 512)
    est = 2 * (_nbytes((D, tn), F32) + _nbytes((Rp, D), F32) + 2 * _nbytes((Rp, tn), F32))
    return pl.pallas_call(
        _ada_body,
        grid=(S, N // tn),
        in_specs=[pl.BlockSpec((Rp, D), lambda s, j: (0, 0)),
                  pl.BlockSpec((1, D, tn), lambda s, j: (s, 0, j)),
                  pl.BlockSpec((1, 1, tn), lambda s, j: (s, 0, j))],
        out_specs=pl.BlockSpec((1, Rp, tn), lambda s, j: (s, 0, j)),
        out_shape=jax.ShapeDtypeStruct((S, Rp, N), F32),
        compiler_params=_params(("parallel", "parallel"), est),
        name="ada_mods",
    )(c_all, ada_w, ada_b)


def _mod_spec(mod, R, tr):
    D = mod.shape[-1]
    if mod.shape[1] == 1:
        return pl.BlockSpec((1, 1, D), lambda b, r: (b, 0, 0))
    assert mod.shape[1] == R
    return pl.BlockSpec((1, tr, D), lambda b, r: (b, r, 0))


def _prep_plain_body(x_ref, g_ref, sh_ref, sc_ref, h_ref):
    h_ref[0] = _rms_mod(x_ref[0], g_ref[...], sh_ref[0], sc_ref[0]).astype(h_ref.dtype)


def _prep_plain(x, g, shift, scale):
    Bx, R, D = x.shape
    tr = _div_tile(R, 256)
    est = 2 * (_nbytes((tr, D), F32) * 3 + _nbytes((tr, D), BF16))
    return pl.pallas_call(
        _prep_plain_body,
        grid=(Bx, R // tr),
        in_specs=[pl.BlockSpec((1, tr, D), lambda b, r: (b, r, 0)),
                  pl.BlockSpec((1, D), lambda b, r: (0, 0)),
                  _mod_spec(shift, R, tr), _mod_spec(scale, R, tr)],
        out_specs=pl.BlockSpec((1, tr, D), lambda b, r: (b, r, 0)),
        out_shape=jax.ShapeDtypeStruct((Bx, R, D), BF16),
        compiler_params=_params(("parallel", "parallel"), est),
        name="prep_plain",
    )(x, g, shift, scale)


def _prep_moe_body(n_exp, x_ref, g_ref, sh_ref, sc_ref, wr_ref, h_ref, gates_ref, sel_ref):
    h = _rms_mod(x_ref[0], g_ref[...], sh_ref[0], sc_ref[0])
    h_ref[0] = h.astype(h_ref.dtype)
    logits = _dot(h, wr_ref[...])
    lane = lax.broadcasted_iota(jnp.int32, logits.shape, 1)
    logits = jnp.where(lane < n_exp, logits, NEG_INF)
    e = jnp.exp(logits - jnp.max(logits, axis=-1, keepdims=True))
    p = e / jnp.sum(e, axis=-1, keepdims=True)
    p = jnp.where(lane < n_exp, p, -1.0)
    lane_f = lane.astype(F32)
    m1 = jnp.max(p, axis=-1, keepdims=True)
    i1 = jnp.min(jnp.where(p == m1, lane_f, float(LANE)), axis=-1, keepdims=True)
    p2 = jnp.where(lane_f == i1, -1.0, p)
    m2 = jnp.max(p2, axis=-1, keepdims=True)
    i2 = jnp.min(jnp.where(p2 == m2, lane_f, float(LANE)), axis=-1, keepdims=True)
    tot = m1 + m2
    gates_ref[0] = jnp.where(lane_f == i1, m1 / tot, jnp.where(lane_f == i2, m2 / tot, 0.0))
    sel_ref[0] = jnp.where((lane_f == i1) | (lane_f == i2), 1.0, 0.0)


def _prep_moe(x, g, shift, scale, w_router_pad, n_exp, h_dtype):
    Bx, R, D = x.shape
    tr = _div_tile(R, 256)
    est = 2 * (_nbytes((tr, D), F32) * 4 + _nbytes((D, LANE), F32))
    return pl.pallas_call(
        functools.partial(_prep_moe_body, n_exp),
        grid=(Bx, R // tr),
        in_specs=[pl.BlockSpec((1, tr, D), lambda b, r: (b, r, 0)),
                  pl.BlockSpec((1, D), lambda b, r: (0, 0)),
                  _mod_spec(shift, R, tr), _mod_spec(scale, R, tr),
                  pl.BlockSpec((D, LANE), lambda b, r: (0, 0))],
        out_specs=[pl.BlockSpec((1, tr, D), lambda b, r: (b, r, 0)),
                   pl.BlockSpec((1, tr, LANE), lambda b, r: (b, r, 0)),
                   pl.BlockSpec((1, tr, LANE), lambda b, r: (b, r, 0))],
        out_shape=[jax.ShapeDtypeStruct((Bx, R, D), h_dtype),
                   jax.ShapeDtypeStruct((Bx, R, LANE), F32),
                   jax.ShapeDtypeStruct((Bx, R, LANE), F32)],
        compiler_params=_params(("parallel", "parallel"), est),
        name="prep_moe",
    )(x, g, shift, scale, w_router_pad)


def _prep_rwkv_body(seq, x_ref, g_ref, sh_ref, sc_ref, mu_ref, sp_ref, lerp_ref, hl_ref, carry_ref):
    h = _rms_mod(x_ref[0], g_ref[...], sh_ref[0], sc_ref[0])
    tr = h.shape[0]
    if seq:
        @pl.when(pl.program_id(1) == 0)
        def _():
            carry_ref[...] = sp_ref[0]
        row = lax.broadcasted_iota(jnp.int32, h.shape, 0)
        h_prev = jnp.where(row == 0, carry_ref[...], pltpu.roll(h, 1, 0))
        carry_ref[...] = h[tr - 1:tr, :]
        hl_ref[0] = h[tr - 1:tr, :]
    else:
        h_prev = sp_ref[0]
        hl_ref[0] = h
    xx = h_prev - h
    for i in range(N_LERP):
        lerp_ref[i, 0] = (h + xx * mu_ref[i:i + 1, :]).astype(lerp_ref.dtype)


def _prep_rwkv(x, g, shift, scale, mu, shift_prev, seq):
    Bx, R, D = x.shape
    tr = _div_tile(R, 128)
    est = 2 * (_nbytes((tr, D), F32) * 4 + _nbytes((N_LERP, tr, D), BF16)) + _nbytes((N_LERP, D), F32)
    if seq:
        sp_spec = pl.BlockSpec((1, 1, D), lambda b, r: (b, 0, 0))
        hl_spec = pl.BlockSpec((1, 1, D), lambda b, r: (b, 0, 0))
        hl_shape = jax.ShapeDtypeStruct((Bx, 1, D), F32)
    else:
        sp_spec = pl.BlockSpec((1, tr, D), lambda b, r: (b, r, 0))
        hl_spec = pl.BlockSpec((1, tr, D), lambda b, r: (b, r, 0))
        hl_shape = jax.ShapeDtypeStruct((Bx, R, D), F32)
    return pl.pallas_call(
        functools.partial(_prep_rwkv_body, seq),
        grid=(Bx, R // tr),
        in_specs=[pl.BlockSpec((1, tr, D), lambda b, r: (b, r, 0)),
                  pl.BlockSpec((1, D), lambda b, r: (0, 0)),
                  _mod_spec(shift, R, tr), _mod_spec(scale, R, tr),
                  pl.BlockSpec((N_LERP, D), lambda b, r: (0, 0)),
                  sp_spec],
        out_specs=[pl.BlockSpec((N_LERP, 1, tr, D), lambda b, r: (0, b, r, 0)), hl_spec],
        out_shape=[jax.ShapeDtypeStruct((N_LERP, Bx, R, D), BF16), hl_shape],
        scratch_shapes=[pltpu.VMEM((1, D), F32)],
        compiler_params=_params(("parallel", "arbitrary"), est),
        name="prep_rwkv",
    )(x, g, shift, scale, mu, shift_prev)


def _linear_body(n_w, n_ex, n_pf, n_into, epilogue, is_padding, *refs):
    pf_refs, refs = refs[:n_pf], refs[n_pf:]
    a_ref = refs[0]
    w_refs = refs[1:1 + n_w]
    ex_refs = refs[1 + n_w:1 + n_w + n_ex]
    out_refs = refs[1 + n_w + n_ex + n_into:]

    def compute():
        a = a_ref[...].reshape(a_ref.shape[-2:])
        accs = [_dot(a, w[...].reshape(w.shape[-2:])) for w in w_refs]
        exs = [e[...].reshape(e.shape[-2:]) for e in ex_refs]
        res = epilogue(*accs, *exs)
        if not isinstance(res, (tuple, list)):
            res = (res,)
        for o, v in zip(out_refs, res):
            o[...] = v.astype(o.dtype).reshape(o.shape)

    if is_padding is None:
        compute()
    else:
        pad = is_padding(*pf_refs)
        pl.when(jnp.logical_not(pad))(compute)

        @pl.when(pad)
        def _():
            for o in out_refs:
                o[...] = jnp.zeros(o.shape, o.dtype)


def _linear(grid, a, a_spec, ws, w_spec, extras, outs, epilogue, sem, prefetch=None, into=None,
            is_padding=None, name="linear"):
    def blk_bytes(spec, dtype):
        dims = [d.block_size if isinstance(d, pl.Element) else (1 if d is None else d)
                for d in spec.block_shape]
        return _nbytes(dims, dtype)

    est = 2 * blk_bytes(a_spec, a.dtype)
    est += sum(2 * blk_bytes(w_spec, w.dtype) + blk_bytes(w_spec, BF16) for w in ws)
    est += sum(2 * blk_bytes(s, e.dtype) for e, s in extras)
    est += sum(4 * blk_bytes(s, F32) for o, s in outs)
    n_pf = 0 if prefetch is None else 1
    body = functools.partial(_linear_body, len(ws), len(extras), n_pf, 0 if into is None else 1, epilogue,
                             is_padding)
    in_specs = [a_spec] + [w_spec] * len(ws) + [s for _, s in extras]
    out_specs = [s for _, s in outs]
    out_shape = [o for o, _ in outs]
    args = [a] + list(ws) + [e for e, _ in extras]
    aliases = {}
    if into is not None:
        aliases = {n_pf + len(args): 0}
        in_specs.append(pl.BlockSpec(memory_space=pl.ANY))
        args.append(into)
    if prefetch is None:
        call = pl.pallas_call(body, grid=grid, in_specs=in_specs, out_specs=out_specs,
                              out_shape=out_shape, input_output_aliases=aliases,
                              compiler_params=_params(sem, est), name=name)
        res = call(*args)
    else:
        gs = pltpu.PrefetchScalarGridSpec(num_scalar_prefetch=1, grid=grid, in_specs=in_specs,
                                          out_specs=out_specs)
        call = pl.pallas_call(body, grid_spec=gs, out_shape=out_shape, input_output_aliases=aliases,
                              compiler_params=_params(sem, est), name=name)
        res = call(prefetch, *args)
    return res


def _row_tile(M):
    return _div_tile(M, 1024)


def _gate_spec_ij(gate, rows_per_batch, tm, tn):
    if gate.shape[1] == 1:
        return pl.BlockSpec((1, 1, tn), lambda i, j: ((i * tm) // rows_per_batch, 0, j))
    assert gate.shape[0] == 1 and gate.shape[1] == tm
    return pl.BlockSpec((1, tm, tn), lambda i, j: (0, 0, j))


def _proj(a, w, w_lead, *, a_lead=0, n_cols=None, epilogue=None, extras=(), out_dtype=F32, tn_pref=512,
          name="proj"):
    _, M, K = a.shape
    N = w.shape[2] if n_cols is None else n_cols
    tm = _row_tile(M)
    tn = _div_tile(N, tn_pref if M > 64 else 1024)
    ex = []
    for e in extras:
        if e.shape[0] == 1:
            ex.append((e, pl.BlockSpec((1, tn), lambda i, j: (0, j))))
        else:
            ex.append((e, pl.BlockSpec((tm, tn), lambda i, j: (i, j))))
    epi = epilogue if epilogue is not None else (lambda acc: acc)
    (out,) = _linear(
        (M // tm, N // tn), a,
        pl.BlockSpec((1, tm, K), lambda i, j: (a_lead, i, 0)),
        [w], pl.BlockSpec((1, K, tn), lambda i, j: (w_lead, 0, j)),
        ex,
        [(jax.ShapeDtypeStruct((M, N), out_dtype), pl.BlockSpec((tm, tn), lambda i, j: (i, j)))],
        epi, ("parallel", "arbitrary"), name=name)
    return out


def _proj_residual(a, w, w_lead, x, gate, rows_per_batch, *, k_blk=0, k_size=None, prev=None,
                   final=True, name="proj_res"):
    M = a.shape[1]
    K = a.shape[2] if k_size is None else k_size
    N = w.shape[2]
    tm = _row_tile(min(M, rows_per_batch) if rows_per_batch > 1 else M)
    tn = _div_tile(N, 256 if M > 64 else 512)
    tile = lambda arr: (arr, pl.BlockSpec((tm, tn), lambda i, j: (i, j)))
    ex = []
    if prev is not None:
        ex.append(tile(prev))
    if final:
        ex.append(tile(x))
        ex.append((gate, _gate_spec_ij(gate, rows_per_batch, tm, tn)))

    def epi(acc, *e):
        e = list(e)
        if prev is not None:
            acc = acc + e.pop(0)
        if final:
            xv, gv = e
            acc = xv + gv * acc
        return acc

    (out,) = _linear(
        (M // tm, N // tn), a,
        pl.BlockSpec((1, tm, K), lambda i, j: (0, i, k_blk)),
        [w], pl.BlockSpec((1, K, tn), lambda i, j: (w_lead, k_blk, j)),
        ex,
        [(jax.ShapeDtypeStruct((M, N), F32), pl.BlockSpec((tm, tn), lambda i, j: (i, j)))],
        epi, ("parallel", "arbitrary"), name=name)
    return out


def _scan_body(C, n_par, n_chunks, has_vres, *refs):
    if has_vres:
        (r_ref, k_ref, v_ref, w_ref, a_ref, g_ref, vg_ref, vf_ref,
         kk_ref, ka_ref, rk_ref, lnw_ref, lnb_ref, s0_ref, z_ref, st_ref, s_scr) = refs
    else:
        (r_ref, k_ref, v_ref, w_ref, a_ref, g_ref,
         kk_ref, ka_ref, rk_ref, lnw_ref, lnb_ref, s0_ref, z_ref, st_ref, s_scr) = refs
    t_id = pl.program_id(2)

    @pl.when(t_id == 0)
    def _():
        s_scr[...] = s0_ref[0]

    half = LANE // 2
    lane = lax.broadcasted_iota(jnp.int32, (1, LANE), 1)
    m0 = lane < half
    r2 = lax.broadcasted_iota(jnp.int32, (2 * C, 2 * C), 0) % C
    c2 = lax.broadcasted_iota(jnp.int32, (2 * C, 2 * C), 1) % C
    stril = r2 > c2
    tril = r2 >= c2
    tri_c = (lax.broadcasted_iota(jnp.int32, (C, C), 0)
             >= lax.broadcasted_iota(jnp.int32, (C, C), 1)).astype(F32)
    n_levels = max(1, int(math.log2(C)))
    inv_half = 1.0 / half

    def seg_sum(x):
        s0 = jnp.sum(jnp.where(m0, x, 0.0), axis=-1, keepdims=True)
        s1 = jnp.sum(jnp.where(m0, 0.0, x), axis=-1, keepdims=True)
        return jnp.where(m0, s0, s1)

    def stack(x):
        return jnp.concatenate([jnp.where(m0, x, 0.0), jnp.where(m0, 0.0, x)], axis=0)

    def each(f, *lists):
        return [f(*xs) for xs in zip(*lists)]

    lss = [slice(i * LANE, (i + 1) * LANE) for i in range(n_par)]

    def chunk(c, states):
        sl = pl.ds(pl.multiple_of(c * C, C), C)
        S = list(states)
        r = [r_ref[0, 0, sl, ls] for ls in lss]
        k = [k_ref[0, 0, sl, ls] for ls in lss]
        v = [v_ref[0, 0, sl, ls] for ls in lss]
        a = [a_ref[0, sl, ls] for ls in lss]
        logd = [-jnp.exp(-_softplus(-w_ref[0, sl, ls]) - 0.5) for ls in lss]
        if has_vres:
            v = [vi + (vf_ref[0, sl, ls] - vi) * vg_ref[0, sl, ls] for vi, ls in zip(v, lss)]
        cum = each(lambda x: jnp.dot(tri_c, x, precision=HI, preferred_element_type=F32), logd)
        kkr = [ki * kk_ref[:, ls] for ki, ls in zip(k, lss)]
        kk = each(lambda x: x / jnp.maximum(jnp.sqrt(seg_sum(x * x)), 1e-12), kkr)
        k2 = [ki * (1.0 + (ai - 1.0) * ka_ref[:, ls]) for ki, ai, ls in zip(k, a, lss)]
        eg = each(jnp.exp, cum)
        eng = each(lambda x: jnp.exp(-x), cum)
        a_st = each(lambda kki, ci, li: stack(-kki * jnp.exp(ci - li)), kk, cum, logd)
        r_st = each(lambda ri, e: stack(ri * e), r, eg)
        b_st = each(lambda kki, ai, e: stack(kki * ai * e), kk, a, eng)
        k_st = each(lambda ki, e: stack(ki * e), k2, eng)
        v_st = each(stack, v)
        ar = each(lambda x, y: jnp.concatenate([x, y], axis=0), a_st, r_st)
        xb = each(_dot_nt, ar, b_st)
        xk = each(_dot_nt, ar, k_st)
        p = each(lambda x: jnp.where(stril, x[:2 * C], 0.0), xb)
        lak = each(lambda x: jnp.where(stril, x[:2 * C], 0.0), xk)
        mrb = each(lambda x: jnp.where(tril, x[2 * C:], 0.0), xb)
        mrk = each(lambda x: jnp.where(tril, x[2 * C:], 0.0), xk)
        q = p
        for _ in range(1, n_levels):
            p = each(_dot, p, p)
            q = each(lambda qi, pi: qi + pi + _dot(qi, pi), q, p)
        lv = each(_dot, lak, v_st)
        mv = each(_dot, mrk, v_st)
        a_s = each(_dot_nt, ar, S)
        wm = each(lambda x, y: x[:2 * C] + y, a_s, lv)
        u = each(lambda qi, w: w + _dot(qi, w), q, wm)
        y_st = each(lambda x, m, ui, mvi: x[2 * C:] + _dot(m, ui) + mvi, a_s, mrb, u, mv)
        y = each(lambda x: x[:C] + x[C:], y_st)
        eg_last = each(lambda e: e[C - 1:C, :], eg)
        S_new = each(lambda Si, e, ui, b, vs, ks: Si * e + _dot_tn(ui, b * e) + _dot_tn(vs, ks * e),
                     S, eg_last, u, b_st, v_st, k_st)
        mean = each(lambda x: seg_sum(x) * inv_half, y)
        yc = each(lambda x, m: x - m, y, mean)
        var = each(lambda x: seg_sum(x * x) * inv_half, yc)
        for i, ls in enumerate(lss):
            yn = yc[i] * lax.rsqrt(var[i] + GN_EPS) * lnw_ref[:, ls] + lnb_ref[:, ls]
            bonus = seg_sum(r[i] * k2[i] * rk_ref[:, ls]) * v[i]
            z_ref[0, sl, ls] = ((yn + bonus) * g_ref[0, sl, ls]).astype(z_ref.dtype)
        return tuple(S_new)

    states = lax.fori_loop(0, n_chunks, chunk, tuple(s_scr[i] for i in range(n_par)))
    for i in range(n_par):
        s_scr[i] = states[i]

    @pl.when(t_id == pl.num_programs(2) - 1)
    def _():
        st_ref[0] = s_scr[...]


def _rwkv_scan(rkv, w_pre, a_gate, g, vres, vecs, s0_bd, C):
    _, B, T, D = rkv.shape
    n_par = SCAN_PAIRS if (D // LANE) % SCAN_PAIRS == 0 else 1
    W = n_par * LANE
    nb = D // W
    tb = _div_tile(T, SCAN_ROWS)
    assert tb % C == 0
    seq_spec = pl.BlockSpec((1, tb, W), lambda b, h, t: (b, t, h))
    rkv_spec = lambda i: pl.BlockSpec((1, 1, tb, W), lambda b, h, t: (i, b, t, h))
    vec_spec = pl.BlockSpec((1, W), lambda b, h, t: (0, h))
    st_spec = pl.BlockSpec((1, n_par, LANE, LANE), lambda b, h, t: (b, h, 0, 0))
    has_vres = vres is not None
    n_seq = 6 + (2 if has_vres else 0)
    in_specs = [rkv_spec(0), rkv_spec(1), rkv_spec(2)] + [seq_spec] * (n_seq - 3) + [vec_spec] * 5 + [st_spec]
    args = [rkv, rkv, rkv, w_pre, a_gate, g] + (list(vres) if has_vres else []) + list(vecs) + [s0_bd]
    est = 2 * (n_seq * _nbytes((tb, W), F32) + _nbytes((tb, W), BF16) + 2 * _nbytes((n_par, LANE, LANE), F32))
    est += n_par * 64 * _nbytes((2 * C, LANE), F32)
    return pl.pallas_call(
        functools.partial(_scan_body, C, n_par, tb // C, has_vres),
        grid=(B, nb, T // tb),
        in_specs=in_specs,
        out_specs=[seq_spec, st_spec],
        out_shape=[jax.ShapeDtypeStruct((B, T, D), BF16),
                   jax.ShapeDtypeStruct((B, D // LANE, LANE, LANE), F32)],
        scratch_shapes=[pltpu.VMEM((n_par, LANE, LANE), F32)],
        compiler_params=_params(("parallel", "parallel", "arbitrary"), est),
        name="rwkv_scan",
    )(*args)


def _to_block_diag(s):
    B, H, N, _ = s.shape
    s = s.reshape(B, H // 2, 2, N, N)
    z = jnp.zeros_like(s[:, :, 0])
    top = jnp.concatenate([s[:, :, 0], z], axis=-1)
    bot = jnp.concatenate([z, s[:, :, 1]], axis=-1)
    return jnp.concatenate([top, bot], axis=-2)


def _from_block_diag(s_bd, N):
    B, nb = s_bd.shape[:2]
    return jnp.stack([s_bd[:, :, :N, :N], s_bd[:, :, N:, N:]], axis=2).reshape(B, 2 * nb, N, N)


def _rwkv_step_body(has_vres, mxu_round, *refs):
    if has_vres:
        (r_ref, k_ref, v_ref, w_ref, a_ref, g_ref, vg_ref, vf_ref,
         kk_ref, ka_ref, rk_ref, lnw_ref, lnb_ref, s0_ref, z_ref, st_ref) = refs
    else:
        (r_ref, k_ref, v_ref, w_ref, a_ref, g_ref,
         kk_ref, ka_ref, rk_ref, lnw_ref, lnb_ref, s0_ref, z_ref, st_ref) = refs
    r, k, v, a = r_ref[0], k_ref[0], v_ref[0], a_ref[0]
    S = s0_ref[0]
    N = S.shape[-1]
    eye = (lax.broadcasted_iota(jnp.int32, (N, N), 0) == lax.broadcasted_iota(jnp.int32, (N, N), 1))
    col = lambda x: jnp.sum(jnp.where(eye, x, 0.0), axis=-1, keepdims=True)
    row = lambda x: jnp.sum(jnp.where(eye, x, 0.0), axis=-2, keepdims=True)
    rnd = (lambda x: x.astype(BF16).astype(F32)) if mxu_round else (lambda x: x)
    decay = jnp.exp(-jnp.exp(-_softplus(-w_ref[0]) - 0.5))
    if has_vres:
        v = v + (vf_ref[0] - v) * vg_ref[0]
    kkr = k * kk_ref[...]
    kk = kkr / jnp.maximum(jnp.sqrt(jnp.sum(kkr * kkr, axis=-1, keepdims=True)), 1e-12)
    k2 = k * (1.0 + (a - 1.0) * ka_ref[...])
    sa = jnp.sum(rnd(S) * rnd(-kk), axis=-1, keepdims=True)
    S_new = S * decay + sa * (kk * a) + col(v) * k2
    st_ref[0] = S_new
    y = jnp.sum(rnd(S_new) * rnd(r), axis=-1, keepdims=True)
    mean = jnp.mean(y, axis=-2, keepdims=True)
    yc = y - mean
    var = jnp.mean(yc * yc, axis=-2, keepdims=True)
    yn = row(yc * lax.rsqrt(var + GN_EPS)) * lnw_ref[...] + lnb_ref[...]
    bonus = jnp.sum(r * k2 * rk_ref[...], axis=-1, keepdims=True) * v
    z_ref[0] = (yn + bonus) * g_ref[0]


def _rwkv_step(seqs, vecs, s0, mxu_round):
    Bd, H, N, _ = s0.shape
    row_spec = pl.BlockSpec((1, H, 1, N), lambda b: (b, 0, 0, 0))
    vec_spec = pl.BlockSpec((H, 1, N), lambda b: (0, 0, 0))
    st_spec = pl.BlockSpec((1, H, N, N), lambda b: (b, 0, 0, 0))
    est = 24 * _nbytes((H, N, LANE), F32)
    return pl.pallas_call(
        functools.partial(_rwkv_step_body, len(seqs) == 8, mxu_round),
        grid=(Bd,),
        in_specs=[row_spec] * len(seqs) + [vec_spec] * 5 + [st_spec],
        out_specs=[row_spec, st_spec],
        out_shape=[jax.ShapeDtypeStruct((Bd, H, 1, N), F32), jax.ShapeDtypeStruct((Bd, H, N, N), F32)],
        compiler_params=_params(("parallel",), est),
        name="rwkv_step",
    )(*seqs, *vecs, s0)


def _cumsum_body(tb, lf_ref, f_ref):
    T = lf_ref.shape[1]
    tri = (lax.broadcasted_iota(jnp.int32, (tb, tb), 0)
           >= lax.broadcasted_iota(jnp.int32, (tb, tb), 1)).astype(F32)

    def blk(i, carry):
        sl = pl.ds(pl.multiple_of(i * tb, tb), tb)
        f = jnp.dot(tri, lf_ref[0, sl, :], precision=HI, preferred_element_type=F32) + carry
        f_ref[0, sl, :] = f
        return f[tb - 1:tb, :]

    lax.fori_loop(0, T // tb, blk, jnp.zeros((1, lf_ref.shape[2]), F32))


def _cumsum_rows(lf):
    B, T, W = lf.shape
    tb = _div_tile(T, 256)
    return pl.pallas_call(
        functools.partial(_cumsum_body, tb),
        grid=(B,),
        in_specs=[pl.BlockSpec((1, T, W), lambda b: (b, 0, 0))],
        out_specs=pl.BlockSpec((1, T, W), lambda b: (b, 0, 0)),
        out_shape=jax.ShapeDtypeStruct((B, T, W), F32),
        compiler_params=_params(("parallel",), 4 * _nbytes((T, W), F32)),
        name="logf_cumsum",
    )(lf)


def _fox_prefill_body(tq, n_h, scale, q_ref, k_ref, v_ref, fr_ref, fc_ref, o_ref, kb_ref, vb_ref):
    T = q_ref.shape[2]
    kb_ref[...] = k_ref[0, 0].astype(BF16)
    vb_ref[...] = v_ref[0, 0].astype(BF16)
    causal = (lax.broadcasted_iota(jnp.int32, (tq, tq), 1)
              <= lax.broadcasted_iota(jnp.int32, (tq, tq), 0))
    lss = [slice(i * LANE, (i + 1) * LANE) for i in range(n_h)]
    heads = range(n_h)

    def q_blk(qi, _):
        qs = pl.ds(pl.multiple_of(qi * tq, tq), tq)
        q = [q_ref[0, 0, qs, ls].astype(BF16) for ls in lss]
        fq = [fc_ref[0, i, qs, :] for i in heads]

        def kv_blk(ks, carry, diagonal):
            m, l, acc = carry
            s = [_dot_nt(q[i], kb_ref[ks, lss[i]]) * scale + fq[i] - fr_ref[0, i, :, ks] for i in heads]
            if diagonal:
                s = [jnp.where(causal, x, NEG_INF) for x in s]
            m_new = [jnp.maximum(m[i], jnp.max(s[i], axis=-1, keepdims=True)) for i in heads]
            alpha = [jnp.exp(m[i] - m_new[i]) for i in heads]
            p = [jnp.exp(s[i] - m_new[i]) for i in heads]
            l = [alpha[i] * l[i] + jnp.sum(p[i], axis=-1, keepdims=True) for i in heads]
            acc = [alpha[i] * acc[i] + _dot(p[i], vb_ref[ks, lss[i]]) for i in heads]
            return tuple(m_new), tuple(l), tuple(acc)

        init = (tuple(jnp.full((tq, 1), NEG_INF, F32) for _ in heads),
                tuple(jnp.zeros((tq, 1), F32) for _ in heads),
                tuple(jnp.zeros((tq, LANE), F32) for _ in heads))
        carry = lax.fori_loop(
            0, qi, lambda kj, c: kv_blk(pl.ds(pl.multiple_of(kj * tq, tq), tq), c, False), init)
        _, l, acc = kv_blk(qs, carry, True)
        for i in heads:
            o_ref[0, qs, lss[i]] = (acc[i] / l[i]).astype(o_ref.dtype)
        return 0

    lax.fori_loop(0, T // tq, q_blk, 0)


def _fox_prefill(qk, v, f_row, f_col, head_dim):
    _, B, T, D = qk.shape
    H = D // head_dim
    assert head_dim == LANE
    tq = _div_tile(T, 512)
    n_h = PREFILL_HEADS if H % PREFILL_HEADS == 0 else 1
    W = n_h * LANE
    spec = lambda i: pl.BlockSpec((1, 1, T, W), lambda b, h: (i, b, 0, h))
    est = 2 * (4 * _nbytes((T, W), F32) + n_h * _nbytes((T, LANE), F32) + 2 * _nbytes((T, W), BF16))
    est += 6 * n_h * _nbytes((tq, tq), F32)
    return pl.pallas_call(
        functools.partial(_fox_prefill_body, tq, n_h, head_dim ** -0.5),
        grid=(B, H // n_h),
        in_specs=[spec(0), spec(1), spec(0),
                  pl.BlockSpec((1, n_h, 1, T), lambda b, h: (b, h, 0, 0)),
                  pl.BlockSpec((1, n_h, T, 1), lambda b, h: (b, h, 0, 0))],
        out_specs=pl.BlockSpec((1, T, W), lambda b, h: (b, 0, h)),
        out_shape=jax.ShapeDtypeStruct((B, T, D), BF16),
        scratch_shapes=[pltpu.VMEM((T, W), BF16), pltpu.VMEM((T, W), BF16)],
        compiler_params=_params(("parallel", "parallel"), est),
        name="fox_prefill",
    )(qk, qk, v, f_row, f_col)


def _fox_decode_body(scale, pt_ref, q_ref, kn_ref, vn_ref, lfn_ref, kp_ref, vp_ref, lfp_ref,
                     o_ref, m_ref, l_ref, acc_ref, carry_ref):
    p_id = pl.program_id(1)
    n_pages = pl.num_programs(1)
    P, n_grp = kp_ref.shape[0], kp_ref.shape[1]
    W = SUBLANE * P

    @pl.when(p_id == 0)
    def _():
        m_ref[...] = jnp.full(m_ref.shape, NEG_INF, F32)
        l_ref[...] = jnp.zeros(l_ref.shape, F32)
        acc_ref[...] = jnp.zeros(acc_ref.shape, F32)
        carry_ref[...] = lfn_ref[0]

    lf = lfp_ref[...]
    pos_r = lax.broadcasted_iota(jnp.int32, (P, W + LANE), 0)
    pos_c = lax.broadcasted_iota(jnp.int32, (P, W + LANE), 1)
    sums = lax.dot_general(lf, ((pos_r > pos_c // SUBLANE) | (pos_c >= W)).astype(F32),
                           (((0,), (0,)), ((), ())), precision=HI, preferred_element_type=F32)
    bias = sums[:, :W] + carry_ref[...]
    carry_ref[...] = carry_ref[...] + sums[:, W:W + 1]

    own = (lax.broadcasted_iota(jnp.int32, (SUBLANE, W), 1) % SUBLANE
           == lax.broadcasted_iota(jnp.int32, (SUBLANE, W), 0))
    q = q_ref[0]
    s_grp = []
    for g in range(n_grp):
        rows = slice(g * SUBLANE, (g + 1) * SUBLANE)
        sg = _dot_nt(q[rows], kp_ref[:, g].reshape(W, LANE))
        s_grp.append(jnp.where(own, sg * scale + bias[rows], NEG_INF))
    s = jnp.concatenate(s_grp, axis=0)
    m_new = jnp.maximum(m_ref[...], jnp.max(s, axis=-1, keepdims=True))
    alpha = jnp.exp(m_ref[...] - m_new)
    p = jnp.exp(s - m_new)
    l_ref[...] = alpha * l_ref[...] + jnp.sum(p, axis=-1, keepdims=True)
    m_ref[...] = m_new
    pv = [_dot(p[g * SUBLANE:(g + 1) * SUBLANE], vp_ref[:, g].reshape(W, LANE)) for g in range(n_grp)]
    acc_ref[...] = alpha * acc_ref[...] + jnp.concatenate(pv, axis=0)

    @pl.when(p_id == n_pages - 1)
    def _():
        s_new = jnp.sum(q_ref[0] * kn_ref[0], axis=-1, keepdims=True) * scale
        m_fin = jnp.maximum(m_ref[...], s_new)
        al = jnp.exp(m_ref[...] - m_fin)
        p_new = jnp.exp(s_new - m_fin)
        l_fin = al * l_ref[...] + p_new
        o_ref[0] = ((al * acc_ref[...] + p_new * vn_ref[0]) / l_fin).astype(o_ref.dtype)


def _fox_decode(q, k_new, v_new, lf_new, cache_k, cache_v, cache_lf, layer, page_table):
    Bd, H, hd = q.shape
    assert hd == LANE and H % SUBLANE == 0
    P = cache_lf.shape[2]
    G = H // SUBLANE
    n_pages = page_table.shape[1]
    row = lambda: pl.BlockSpec((1, H, hd), lambda b, p, pt: (b, 0, 0))
    page_idx = lambda b, p, pt: (layer, pt[b, n_pages - 1 - p], 0, 0, 0, 0)
    gs = pltpu.PrefetchScalarGridSpec(
        num_scalar_prefetch=1,
        grid=(Bd, n_pages),
        in_specs=[row(), row(), row(),
                  pl.BlockSpec((1, H, 1), lambda b, p, pt: (b, 0, 0)),
                  pl.BlockSpec((None, None, P, G, SUBLANE, hd), page_idx),
                  pl.BlockSpec((None, None, P, G, SUBLANE, hd), page_idx),
                  pl.BlockSpec((None, None, P, H), lambda b, p, pt: (layer, pt[b, n_pages - 1 - p], 0, 0))],
        out_specs=pl.BlockSpec((1, H, hd), lambda b, p, pt: (b, 0, 0)),
        scratch_shapes=[pltpu.VMEM((H, 1), F32), pltpu.VMEM((H, 1), F32),
                        pltpu.VMEM((H, LANE), F32), pltpu.VMEM((H, 1), F32)])
    est = 4 * _nbytes((P * H, hd), F32) + 8 * _nbytes((H, SUBLANE * P), F32) + 2 * MIB
    return pl.pallas_call(
        functools.partial(_fox_decode_body, hd ** -0.5),
        grid_spec=gs,
        out_shape=jax.ShapeDtypeStruct((Bd, H, hd), BF16),
        compiler_params=_params(("parallel", "arbitrary"), est),
        name="fox_decode",
    )(page_table, q, k_new, v_new, lf_new, cache_k, cache_v, cache_lf)


def _head_rms(acc, g_row, head_dim):
    tm, tn = acc.shape
    parts = []
    for h in range(tn // head_dim):
        blk = acc[:, h * head_dim:(h + 1) * head_dim]
        blk = blk * lax.rsqrt(jnp.mean(blk * blk, axis=-1, keepdims=True) + NORM_EPS)
        parts.append(blk * g_row)
    return parts[0] if len(parts) == 1 else jnp.concatenate(parts, axis=-1)


def _qkv_proj(h, w_qkvf, layer, qk_gain, head_dim):
    _, M, D = h.shape
    tm = _row_tile(M)
    tn = _div_tile(D, 512 if M > 64 else 1024)
    nj = D // tn

    def epi(acc, gq):
        return _head_rms(acc, gq, head_dim)

    (qk,) = _linear(
        (M // tm, 2 * nj), h,
        pl.BlockSpec((1, tm, D), lambda i, j: (0, i, 0)),
        [w_qkvf], pl.BlockSpec((1, D, tn), lambda i, j: (layer, 0, j)),
        [(qk_gain, pl.BlockSpec((1, 1, head_dim), lambda i, j: (j // nj, 0, 0)))],
        [(jax.ShapeDtypeStruct((2, M, D), F32),
          pl.BlockSpec((1, tm, tn), lambda i, j: (j // nj, i, j % nj)))],
        epi, ("parallel", "arbitrary"), name="qk_proj")
    (v,) = _linear(
        (M // tm, nj), h,
        pl.BlockSpec((1, tm, D), lambda i, j: (0, i, 0)),
        [w_qkvf], pl.BlockSpec((1, D, tn), lambda i, j: (layer, 0, 2 * nj + j)),
        [],
        [(jax.ShapeDtypeStruct((1, M, D), F32), pl.BlockSpec((1, tm, tn), lambda i, j: (0, i, j)))],
        lambda acc: acc, ("parallel", "arbitrary"), name="v_proj")
    return qk, v


def _swiglu_epi(g, u):
    return (g * jax.nn.sigmoid(g)) * u


def _col_segments(F, tn):
    assert F % LANE == 0 and tn % LANE == 0
    n = F // tn
    segs = [(0, tn, n)] if n else []
    if F - n * tn:
        segs.append((n * tn, F - n * tn, 1))
    return segs


def _ffn_up_dense(h, w_gate, w_up, layer):
    _, M, D = h.shape
    F = w_gate.shape[2]
    tm = _row_tile(M)
    E = pl.Element
    segs = _col_segments(F, 256 if M > 64 else 512)
    out = jnp.zeros((M, F), BF16) if len(segs) > 1 else None
    for c0, tn, n_j in segs:
        off = lambda j: pl.multiple_of(c0 + j * tn, LANE)
        (out,) = _linear(
            (M // tm, n_j), h,
            pl.BlockSpec((1, tm, D), lambda i, j: (0, i, 0)),
            [w_gate, w_up], pl.BlockSpec((None, E(D), E(tn)), lambda i, j: (layer, 0, off(j))),
            [],
            [(jax.ShapeDtypeStruct((M, F), BF16),
              pl.BlockSpec((E(tm), E(tn)), lambda i, j: (i * tm, off(j))))],
            _swiglu_epi, ("parallel", "arbitrary"), into=out, name="ffn_up")
    return out


def _moe_up_sorted(a_sorted, w_gate, w_up, layer_base, n_exp, tile_expert, tm):
    _, Mp, D = a_sorted.shape
    F = w_gate.shape[2]
    E = pl.Element
    segs = _col_segments(F, 512)
    out = jnp.zeros((Mp, F), BF16) if len(segs) > 1 else None
    for c0, tn, n_j in segs:
        off = lambda j: pl.multiple_of(c0 + j * tn, LANE)
        (out,) = _linear(
            (n_j, Mp // tm), a_sorted,
            pl.BlockSpec((1, tm, D), lambda j, s, te: (0, s, 0)),
            [w_gate, w_up],
            pl.BlockSpec((None, E(D), E(tn)), lambda j, s, te: (layer_base + te[s] % n_exp, 0, off(j))),
            [],
            [(jax.ShapeDtypeStruct((Mp, F), BF16),
              pl.BlockSpec((E(tm), E(tn)), lambda j, s, te: (s * tm, off(j))))],
            _swiglu_epi, ("parallel", "arbitrary"), prefetch=tile_expert, into=out,
            is_padding=lambda te: te[pl.program_id(1)] >= n_exp, name="moe_up")
    return out


def _moe_down_sorted(a_sorted, w_down, layer_base, n_exp, tile_expert, pair_gate, tm):
    _, Mp, F = a_sorted.shape
    D = w_down.shape[2]
    tn = _div_tile(D, 512)
    (out,) = _linear(
        (D // tn, Mp // tm), a_sorted,
        pl.BlockSpec((1, tm, F), lambda j, s, te: (0, s, 0)),
        [w_down], pl.BlockSpec((1, F, tn), lambda j, s, te: (layer_base + te[s] % n_exp, 0, j)),
        [(pair_gate, pl.BlockSpec((tm, 1), lambda j, s, te: (s, 0)))],
        [(jax.ShapeDtypeStruct((Mp, D), F32), pl.BlockSpec((tm, tn), lambda j, s, te: (s, j)))],
        lambda acc, pg: acc * pg, ("parallel", "arbitrary"), prefetch=tile_expert,
        is_padding=lambda te: te[pl.program_id(1)] >= n_exp, name="moe_down")
    return out


def _row_copy(src_ref, row, dst_ref, sem):
    return pltpu.make_async_copy(src_ref.at[pl.ds(row, 1), :], dst_ref, sem)


def _gather_rows_body(n_rows, idx_ref, src_ref, out_ref, buf_ref, sem_ref):
    s = pl.program_id(0)
    slot = s % 2

    def issue(step, to_slot):
        def one(r, carry):
            _row_copy(src_ref, idx_ref[step * n_rows + r], buf_ref.at[to_slot, pl.ds(r, 1), :],
                      sem_ref.at[to_slot]).start()
            return carry
        lax.fori_loop(0, n_rows, one, 0)

    @pl.when(s == 0)
    def _():
        issue(0, 0)

    @pl.when(s + 1 < pl.num_programs(0))
    def _():
        issue(s + 1, 1 - slot)

    def wait_one(r, carry):
        _row_copy(src_ref, 0, buf_ref.at[slot, pl.ds(r, 1), :], sem_ref.at[slot]).wait()
        return carry
    lax.fori_loop(0, n_rows, wait_one, 0)
    out_ref[...] = buf_ref[slot].astype(out_ref.dtype)


def _gather_rows(src, idx, n_rows=256):
    M, D = src.shape
    n_out = idx.shape[0]
    assert n_out % n_rows == 0
    gs = pltpu.PrefetchScalarGridSpec(
        num_scalar_prefetch=1, grid=(n_out // n_rows,),
        in_specs=[pl.BlockSpec(memory_space=pl.ANY)],
        out_specs=pl.BlockSpec((n_rows, D), lambda s, ix: (s, 0)),
        scratch_shapes=[pltpu.VMEM((2, n_rows, D), F32), pltpu.SemaphoreType.DMA((2,))])
    est = 2 * _nbytes((n_rows, D), F32) + 3 * _nbytes((n_rows, D), BF16)
    return pl.pallas_call(
        functools.partial(_gather_rows_body, n_rows),
        grid_spec=gs,
        out_shape=jax.ShapeDtypeStruct((n_out, D), BF16),
        compiler_params=_params(("arbitrary",), est),
        name="moe_gather",
    )(idx, src)


def _combine_body(n_tok, pos_ref, y_ref, x_ref, gate_ref, out_ref, buf_ref, sem_ref):
    n_inner = pl.num_programs(1)
    s = pl.program_id(0) * n_inner + pl.program_id(1)
    n_steps = pl.num_programs(0) * n_inner
    slot = s % 2

    def issue(step, to_slot):
        def one(t, carry):
            for k in range(TOP_K):
                _row_copy(y_ref, pos_ref[(step * n_tok + t) * TOP_K + k],
                          buf_ref.at[to_slot, k, pl.ds(t, 1), :], sem_ref.at[to_slot]).start()
            return carry
        lax.fori_loop(0, n_tok, one, 0)

    @pl.when(s == 0)
    def _():
        issue(0, 0)

    @pl.when(s + 1 < n_steps)
    def _():
        issue(s + 1, 1 - slot)

    def wait_one(t, carry):
        for k in range(TOP_K):
            _row_copy(y_ref, 0, buf_ref.at[slot, k, pl.ds(t, 1), :], sem_ref.at[slot]).wait()
        return carry
    lax.fori_loop(0, n_tok, wait_one, 0)
    acc = buf_ref[slot, 0]
    for k in range(1, TOP_K):
        acc = acc + buf_ref[slot, k]
    out_ref[0] = x_ref[0] + gate_ref[0] * acc


def _moe_combine(y_sorted, pos, x, gate, n_tok=128):
    Bx, R, D = x.shape
    n_tok = _div_tile(R, n_tok)
    if gate.shape[1] == 1:
        gate_spec = pl.BlockSpec((1, 1, D), lambda b, s, ps: (b, 0, 0))
    else:
        gate_spec = pl.BlockSpec((1, n_tok, D), lambda b, s, ps: (b, s, 0))
    gs = pltpu.PrefetchScalarGridSpec(
        num_scalar_prefetch=1, grid=(Bx, R // n_tok),
        in_specs=[pl.BlockSpec(memory_space=pl.ANY),
                  pl.BlockSpec((1, n_tok, D), lambda b, s, ps: (b, s, 0)),
                  gate_spec],
        out_specs=pl.BlockSpec((1, n_tok, D), lambda b, s, ps: (b, s, 0)),
        scratch_shapes=[pltpu.VMEM((2, TOP_K, n_tok, D), F32), pltpu.SemaphoreType.DMA((2,))])
    est = (2 * TOP_K + 6) * _nbytes((n_tok, D), F32)
    return pl.pallas_call(
        functools.partial(_combine_body, n_tok),
        grid_spec=gs,
        out_shape=jax.ShapeDtypeStruct((Bx, R, D), F32),
        compiler_params=_params(("arbitrary", "arbitrary"), est),
        name="moe_combine",
    )(pos, y_sorted, x, gate)


def _route_metadata(gates, selmask, n_exp, tm):
    M = gates.shape[0]
    n_tiles = (M * TOP_K) // tm + n_exp
    sel = selmask > 0.5
    seli = sel.astype(jnp.int32)
    rank = jnp.cumsum(seli, axis=0) - seli
    counts = jnp.sum(seli, axis=0)
    tiles_per = (counts + tm - 1) // tm
    tile_start = jnp.cumsum(tiles_per) - tiles_per
    pos = tile_start[None, :] * tm + rank
    tile_ids = jnp.arange(n_tiles, dtype=jnp.int32)
    tile_expert = jnp.clip(jnp.searchsorted(jnp.cumsum(tiles_per), tile_ids, side="right"),
                           0, n_exp - 1).astype(jnp.int32)
    tile_expert = jnp.where(tile_ids < jnp.sum(tiles_per), tile_expert, tile_expert + n_exp)
    flat_pos = jnp.where(sel, pos, n_tiles * tm).reshape(-1)
    pair = jnp.zeros((n_tiles * tm,), jnp.int32).at[flat_pos].set(
        jnp.arange(1, M * n_exp + 1, dtype=jnp.int32), mode="drop")
    pair_idx = jnp.maximum(pair - 1, 0)
    row_src = pair_idx // n_exp
    row_gate = jnp.where(pair > 0, gates.reshape(-1)[pair_idx], 0.0)
    order = jnp.argsort(jnp.where(sel, 0, 1), axis=1, stable=True)[:, :TOP_K]
    tok_pos = jnp.take_along_axis(pos, order, axis=1).astype(jnp.int32).reshape(-1)
    return row_src, tile_expert, row_gate.reshape(-1, 1), tok_pos


def _trunk(x, mods, rows_per_batch, seq, P, shift_in, wkv_in, paged):
    Bx, R, D = x.shape
    M = Bx * R
    depth = P["norm_g"].shape[0]
    n_rwkv_heads, n_state = P["state_dims"]
    head_dim = P["attn_q_g"].shape[-1]
    n_attn_heads = D // head_dim
    n_exp = P["moe_w_router"].shape[-1]
    lo_w, lo_a, lo_g = P["rwkv_w1"].shape[-1], P["rwkv_a1"].shape[-1], P["rwkv_g1"].shape[-1]
    lo_v = P["rwkv_v1"].shape[-1]
    n_seqs = Bx if seq else R
    T = R if seq else 1
    flat = lambda t: t.reshape(1, M, t.shape[-1])

    shifts, wkvs, ks, vs, lfs = [], [], [], [], []
    v_first = None
    for i in range(depth):
        j = i // 2
        shift, scale, gate = mods[(i, 0)]
        g_row = P["norm_g"][i, 0][None, :]
        if i % 2 == 0:
            mu = P["rwkv_mu"][j][jnp.array([0, 2, 3, 1, 4, 5])]
            if seq:
                sp = jnp.zeros((Bx, 1, D), F32) if shift_in is None else shift_in[j][:, None, :]
            else:
                sp = shift_in[j][None]
            lerps, h_keep = _prep_rwkv(x, g_row, shift, scale, mu, sp, seq)
            lerps = lerps.reshape(N_LERP, M, D)
            shifts.append(h_keep.reshape(n_seqs, D))
            n_l = P["rwkv_w_rkv"].shape[0]
            w_rkv = P["rwkv_w_rkv"].reshape(n_l * 3, D, D)
            tm = _row_tile(M)
            tn = _div_tile(D, 512 if M > 64 else 1024)
            (rkv,) = _linear(
                (3, M // tm, D // tn), lerps,
                pl.BlockSpec((1, tm, D), lambda s, a, b: (s, a, 0)),
                [w_rkv], pl.BlockSpec((1, D, tn), lambda s, a, b: (3 * j + s, 0, b)),
                [],
                [(jax.ShapeDtypeStruct((3, M, D), F32), pl.BlockSpec((1, tm, tn), lambda s, a, b: (s, a, b)))],
                lambda acc: acc, ("parallel", "parallel", "arbitrary"), name="rkv_proj")
            row = lambda name: P[name][j][None, :]
            w_mid = _proj(lerps, P["rwkv_w1"], j, a_lead=3, epilogue=jnp.tanh, out_dtype=BF16, name="lora_w1")
            w_pre = _proj(w_mid[None], P["rwkv_w2"], j, extras=[row("rwkv_w0")],
                          epilogue=lambda acc, b: acc + b, name="lora_w2")
            a_mid = _proj(lerps, P["rwkv_a1"], j, a_lead=4, out_dtype=BF16, name="lora_a1")
            a_gate = _proj(a_mid[None], P["rwkv_a2"], j, extras=[row("rwkv_a0")],
                           epilogue=lambda acc, b: jax.nn.sigmoid(acc + b), name="lora_a2")
            g_mid = _proj(lerps, P["rwkv_g1"], j, a_lead=5, epilogue=jax.nn.sigmoid, out_dtype=BF16,
                          name="lora_g1")
            g_out = _proj(g_mid[None], P["rwkv_g2"], j, name="lora_g2")
            if j == 0:
                vres = None
                v_first = rkv[2]
            else:
                v_mid = _proj(lerps, P["rwkv_v1"], j - 1, a_lead=2, out_dtype=BF16, name="lora_v1")
                v_gate = _proj(v_mid[None], P["rwkv_v2"], j - 1, extras=[P["rwkv_v0"][j - 1][None, :]],
                               epilogue=lambda acc, b: jax.nn.sigmoid(acc + b), name="lora_v2")
                vres = (v_gate, v_first)
            s0 = (jnp.zeros((n_seqs, n_rwkv_heads, n_state, n_state), F32) if wkv_in is None else wkv_in[j])
            vecs = (row("rwkv_k_k"), row("rwkv_k_a"), P["rwkv_r_k"][j].reshape(1, D),
                    row("rwkv_ln_w"), row("rwkv_ln_b"))
            if seq:
                shp = lambda t: t.reshape(n_seqs, T, D)
                z, s_bd = _rwkv_scan(rkv.reshape(3, n_seqs, T, D), shp(w_pre), shp(a_gate), shp(g_out),
                                     None if vres is None else tuple(shp(t) for t in vres),
                                     vecs, _to_block_diag(s0), SCAN_CHUNK)
                wkvs.append(_from_block_diag(s_bd, n_state))
                z = z.reshape(1, M, D)
            else:
                hs = lambda t: t.reshape(n_seqs, n_rwkv_heads, 1, n_state)
                seqs = [hs(rkv[0]), hs(rkv[1]), hs(rkv[2]), hs(w_pre), hs(a_gate), hs(g_out)]
                if vres is not None:
                    seqs += [hs(vres[0]), hs(vres[1])]
                z, s_new = _rwkv_step(seqs, [t.reshape(n_rwkv_heads, 1, n_state) for t in vecs], s0,
                                      STEP_MXU_ROUND)
                wkvs.append(s_new)
                z = z.reshape(1, M, D)
            x = _proj_residual(z, P["rwkv_w_o"], j, x.reshape(M, D), gate, rows_per_batch,
                               name="rwkv_out").reshape(Bx, R, D)
        else:
            h = _prep_plain(x, g_row, shift, scale)
            qk, v_att = _qkv_proj(flat(h), P["attn_w_qkvf"], j,
                                  jnp.stack([P["attn_q_g"][j], P["attn_k_g"][j]])[:, None, :], head_dim)
            w_f = jnp.pad(P["attn_w_qkvf"][j][:, 3 * D:], ((0, 0), (0, LANE - n_attn_heads)))[None]
            b_f = jnp.pad(P["attn_b_f"][j], (0, LANE - n_attn_heads))[None, :]
            logf = _proj(flat(h), w_f, 0, extras=[b_f], epilogue=lambda acc, b: -_softplus(-(acc + b)),
                         name="logf_proj")
            ks.append(qk[1].reshape(n_seqs, T, n_attn_heads, head_dim))
            vs.append(v_att[0].reshape(n_seqs, T, n_attn_heads, head_dim))
            lfs.append(logf[:, :n_attn_heads].reshape(n_seqs, T, n_attn_heads))
            if not paged:
                f = _cumsum_rows(logf.reshape(n_seqs, T, LANE))[:, :, :n_attn_heads]
                f_bht = jnp.transpose(f, (0, 2, 1))
                o = _fox_prefill(qk.reshape(2, n_seqs, T, D), v_att.reshape(1, n_seqs, T, D),
                                 f_bht[:, :, None, :], f_bht[:, :, :, None], head_dim)
            else:
                hv = lambda t: t.reshape(n_seqs, n_attn_heads, head_dim)
                o = _fox_decode(hv(qk[0]), hv(qk[1]), hv(v_att[0]),
                                logf[:, :n_attn_heads].reshape(n_seqs, n_attn_heads, 1),
                                P["cache_k"], P["cache_v"], P["cache_lf"], j, P["page_table"])
            x = _proj_residual(o.reshape(1, M, D), P["attn_w_o"], j, x.reshape(M, D), gate, rows_per_batch,
                               name="attn_out").reshape(Bx, R, D)

        shift, scale, gate = mods[(i, 1)]
        g_row = P["norm_g"][i, 1][None, :]
        if i % 2 == 0:
            h = _prep_plain(x, g_row, shift, scale)
            hid = _ffn_up_dense(flat(h), P["ffn_w_gate"], P["ffn_w_up"], j)
            F = hid.shape[1]
            part = _proj_residual(hid[None], P["ffn_w_down"], j, None, None, rows_per_batch,
                                  k_blk=0, k_size=F // 2, final=False, name="ffn_down0")
            x = _proj_residual(hid[None], P["ffn_w_down"], j, x.reshape(M, D), gate, rows_per_batch,
                               k_blk=1, k_size=F // 2, prev=part, name="ffn_down1").reshape(Bx, R, D)
        else:
            x = yield (x, g_row, shift, scale, gate, j)
    return x, ks, vs, lfs, shifts, wkvs


def _moe_joint(requests, P):
    j = requests[0][5]
    g_row = requests[0][1]
    D = requests[0][0].shape[-1]
    n_exp = P["moe_w_router"].shape[-1]
    w_router = jnp.pad(P["moe_w_router"][j], ((0, 0), (0, LANE - n_exp)))
    n_l = P["moe_w_gate"].shape[0]
    wg = P["moe_w_gate"].reshape((n_l * n_exp,) + P["moe_w_gate"].shape[2:])
    wu = P["moe_w_up"].reshape((n_l * n_exp,) + P["moe_w_up"].shape[2:])
    wd = P["moe_w_down"].reshape((n_l * n_exp,) + P["moe_w_down"].shape[2:])
    hs, gs, ss, counts = [], [], [], []
    for x, _, shift, scale, _, _ in requests:
        h32, gates, selmask = _prep_moe(x, g_row, shift, scale, w_router, n_exp, F32)
        m = x.shape[0] * x.shape[1]
        counts.append(m)
        hs.append(h32.reshape(m, D))
        gs.append(gates.reshape(m, LANE)[:, :n_exp])
        ss.append(selmask.reshape(m, LANE)[:, :n_exp])
    tm = _div_tile(max(counts), 512)
    row_src, tile_expert, row_gate, tok_pos = _route_metadata(
        jnp.concatenate(gs), jnp.concatenate(ss), n_exp, tm)
    a_sorted = _gather_rows(jnp.concatenate(hs), row_src)
    hid = _moe_up_sorted(a_sorted[None], wg, wu, j * n_exp, n_exp, tile_expert, tm)
    y = _moe_down_sorted(hid[None], wd, j * n_exp, n_exp, tile_expert, row_gate, tm)
    outs, start = [], 0
    for (x, _, _, _, gate, _), m in zip(requests, counts):
        outs.append(_moe_combine(y, tok_pos[start * TOP_K:(start + m) * TOP_K], x, gate))
        start += m
    return outs


def _run_trunks(trunks, P):
    requests = [next(t) for t in trunks]
    results = [None] * len(trunks)
    while any(r is None for r in results):
        outs = _moe_joint(requests, P)
        for i, t in enumerate(trunks):
            try:
                requests[i] = t.send(outs[i])
            except StopIteration as stop:
                results[i] = stop.value
    return results


def kernel(x_prompt, x_sample, c_prompt, c_sample, cache_k, cache_v, cache_logf, page_table, state_shift, state_wkv, ada_w, ada_b, norm_g, rwkv_mu, rwkv_w_rkv, rwkv_w0, rwkv_w1, rwkv_w2, rwkv_a0, rwkv_a1, rwkv_a2, rwkv_v0, rwkv_v1, rwkv_v2, rwkv_g1, rwkv_g2, rwkv_k_k, rwkv_k_a, rwkv_r_k, rwkv_ln_w, rwkv_ln_b, rwkv_w_o, attn_w_qkvf, attn_b_f, attn_q_g, attn_k_g, attn_w_o, ffn_w_gate, ffn_w_up, ffn_w_down, moe_w_router, moe_w_gate, moe_w_up, moe_w_down):
    B, T, D = x_prompt.shape
    Bd = x_sample.shape[0]
    depth = ada_w.shape[0]
    n_layers_attn, n_pool, page, n_attn_heads, head_dim = cache_k.shape

    P = dict(
        norm_g=norm_g, rwkv_mu=rwkv_mu, rwkv_w_rkv=rwkv_w_rkv, rwkv_w0=rwkv_w0, rwkv_w1=rwkv_w1,
        rwkv_w2=rwkv_w2, rwkv_a0=rwkv_a0, rwkv_a1=rwkv_a1, rwkv_a2=rwkv_a2, rwkv_v0=rwkv_v0,
        rwkv_v1=rwkv_v1, rwkv_v2=rwkv_v2, rwkv_g1=rwkv_g1, rwkv_g2=rwkv_g2, rwkv_k_k=rwkv_k_k,
        rwkv_k_a=rwkv_k_a, rwkv_r_k=rwkv_r_k, rwkv_ln_w=rwkv_ln_w, rwkv_ln_b=rwkv_ln_b,
        rwkv_w_o=rwkv_w_o, attn_w_qkvf=attn_w_qkvf, attn_b_f=attn_b_f, attn_q_g=attn_q_g,
        attn_k_g=attn_k_g, attn_w_o=attn_w_o, ffn_w_gate=ffn_w_gate, ffn_w_up=ffn_w_up,
        ffn_w_down=ffn_w_down, moe_w_router=moe_w_router, moe_w_gate=moe_w_gate, moe_w_up=moe_w_up,
        moe_w_down=moe_w_down,
        state_dims=(state_wkv.shape[2], state_wkv.shape[3]),
        cache_k=cache_k.reshape(n_layers_attn, n_pool, page, n_attn_heads // SUBLANE, SUBLANE, head_dim),
        cache_v=cache_v.reshape(n_layers_attn, n_pool, page, n_attn_heads // SUBLANE, SUBLANE, head_dim),
        cache_lf=cache_logf,
        page_table=page_table,
    )

    n_rows = B + Bd
    rows_pad = -(-n_rows // SUBLANE) * SUBLANE
    c_all = jnp.pad(jnp.concatenate([c_prompt, c_sample], axis=0), ((0, rows_pad - n_rows), (0, 0)))
    mods_all = _ada_mods(c_all, ada_w.reshape(depth * 2, D, 3 * D), ada_b.reshape(depth * 2, 1, 3 * D))

    def split_mods(r0, r1, per_row):
        out = {}
        for i in range(depth):
            for s in range(2):
                m = mods_all[i * 2 + s, r0:r1]
                parts = [m[:, k * D:(k + 1) * D] for k in range(3)]
                out[(i, s)] = tuple(p[None] if per_row else p[:, None, :] for p in parts)
        return out

    (yp, ks, vs, lfs, shifts, wkvs), (ys, ks2, vs2, lfs2, shifts2, wkvs2) = _run_trunks(
        [_trunk(x_prompt, split_mods(0, B, False), T, True, P, None, None, False),
         _trunk(x_sample.reshape(1, Bd, D), split_mods(B, B + Bd, True), 1, False, P, state_shift,
                state_wkv, True)], P)
    return (yp, ys.reshape(Bd, 1, D),
            jnp.stack(ks), jnp.stack(vs), jnp.stack(lfs),
            jnp.stack(ks2), jnp.stack(vs2), jnp.stack(lfs2),
            jnp.stack(shifts), jnp.stack(wkvs), jnp.stack(shifts2), jnp.stack(wkvs2))
```

```python
import functools
import math

import jax
import jax.numpy as jnp
from jax import lax
from jax.experimental import pallas as pl
from jax.experimental.pallas import tpu as pltpu

F32 = jnp.float32
BF16 = jnp.bfloat16

NORM_EPS = 1e-6
GN_EPS = 64e-5
NEG_INF = -1e30
TOP_K = 2
N_LERP = 6

LANE = 128
SUBLANE = 8
VMEM_BYTES = 64 * 1024 * 1024
VMEM_CAP = VMEM_BYTES - 6 * 1024 * 1024
MIB = 1024 * 1024

HI = lax.Precision.HIGHEST

SCAN_CHUNK = 64
SCAN_PAIRS = 16
SCAN_ROWS = 256
PREFILL_HEADS = 4
STEP_MXU_ROUND = True


def _params(sem, est_bytes):
    limit = min(max(int(est_bytes) + 4 * MIB, 32 * MIB), VMEM_CAP)
    return pltpu.CompilerParams(dimension_semantics=sem, vmem_limit_bytes=limit)


def _nbytes(shape, dtype):
    return math.prod(shape) * jnp.dtype(dtype).itemsize


def _div_tile(n, pref):
    if n <= pref:
        return n
    t = pref
    while n % t:
        t //= 2
    return t


def _dot(a, b):
    return jnp.dot(a.astype(BF16), b.astype(BF16), preferred_element_type=F32)


def _dot_nt(a, b):
    return lax.dot_general(a.astype(BF16), b.astype(BF16), (((1,), (1,)), ((), ())),
                           preferred_element_type=F32)


def _dot_tn(a, b):
    return lax.dot_general(a.astype(BF16), b.astype(BF16), (((0,), (0,)), ((), ())),
                           preferred_element_type=F32)


def _softplus(z):
    return jnp.maximum(z, 0.0) + jnp.log1p(jnp.exp(-jnp.abs(z)))


def _rms_mod(x, g, shift, scale):
    y = x * lax.rsqrt(jnp.mean(x * x, axis=-1, keepdims=True) + NORM_EPS)
    return (y * g) * (1.0 + scale) + shift


def _ada_body(c_ref, w_ref, b_ref, o_ref):
    c = c_ref[...]
    a = c * jax.nn.sigmoid(c)
    o_ref[0] = _dot(a, w_ref[0]) + b_ref[0]


def _ada_mods(c_all, ada_w, ada_b):
    S, D, N = ada_w.shape
    Rp = c_all.shape[0]
    tn = _div_tile(N, 512)
    est = 2 * (_nbytes((D, tn), F32) + _nbytes((Rp, D), F32) + 2 * _nbytes((Rp, tn), F32))
    return pl.pallas_call(
        _ada_body,
        grid=(S, N // tn),
        in_specs=[pl.BlockSpec((Rp, D), lambda s, j: (0, 0)),
                  pl.BlockSpec((1, D, tn), lambda s, j: (s, 0, j)),
                  pl.BlockSpec((1, 1, tn), lambda s, j: (s, 0, j))],
        out_specs=pl.BlockSpec((1, Rp, tn), lambda s, j: (s, 0, j)),
        out_shape=jax.ShapeDtypeStruct((S, Rp, N), F32),
        compiler_params=_params(("parallel", "parallel"), est),
        name="ada_mods",
    )(c_all, ada_w, ada_b)


def _mod_spec(mod, R, tr):
    D = mod.shape[-1]
    if mod.shape[1] == 1:
        return pl.BlockSpec((1, 1, D), lambda b, r: (b, 0, 0))
    assert mod.shape[1] == R
    return pl.BlockSpec((1, tr, D), lambda b, r: (b, r, 0))


def _prep_plain_body(x_ref, g_ref, sh_ref, sc_ref, h_ref):
    h_ref[0] = _rms_mod(x_ref[0], g_ref[...], sh_ref[0], sc_ref[0]).astype(h_ref.dtype)


def _prep_plain(x, g, shift, scale):
    Bx, R, D = x.shape
    tr = _div_tile(R, 256)
    est = 2 * (_nbytes((tr, D), F32) * 3 + _nbytes((tr, D), BF16))
    return pl.pallas_call(
        _prep_plain_body,
        grid=(Bx, R // tr),
        in_specs=[pl.BlockSpec((1, tr, D), lambda b, r: (b, r, 0)),
                  pl.BlockSpec((1, D), lambda b, r: (0, 0)),
                  _mod_spec(shift, R, tr), _mod_spec(scale, R, tr)],
        out_specs=pl.BlockSpec((1, tr, D), lambda b, r: (b, r, 0)),
        out_shape=jax.ShapeDtypeStruct((Bx, R, D), BF16),
        compiler_params=_params(("parallel", "parallel"), est),
        name="prep_plain",
    )(x, g, shift, scale)


def _prep_moe_body(n_exp, x_ref, g_ref, sh_ref, sc_ref, wr_ref, h_ref, gates_ref, sel_ref):
    h = _rms_mod(x_ref[0], g_ref[...], sh_ref[0], sc_ref[0])
    h_ref[0] = h.astype(h_ref.dtype)
    logits = _dot(h, wr_ref[...])
    lane = lax.broadcasted_iota(jnp.int32, logits.shape, 1)
    logits = jnp.where(lane < n_exp, logits, NEG_INF)
    e = jnp.exp(logits - jnp.max(logits, axis=-1, keepdims=True))
    p = e / jnp.sum(e, axis=-1, keepdims=True)
    p = jnp.where(lane < n_exp, p, -1.0)
    lane_f = lane.astype(F32)
    m1 = jnp.max(p, axis=-1, keepdims=True)
    i1 = jnp.min(jnp.where(p == m1, lane_f, float(LANE)), axis=-1, keepdims=True)
    p2 = jnp.where(lane_f == i1, -1.0, p)
    m2 = jnp.max(p2, axis=-1, keepdims=True)
    i2 = jnp.min(jnp.where(p2 == m2, lane_f, float(LANE)), axis=-1, keepdims=True)
    tot = m1 + m2
    gates_ref[0] = jnp.where(lane_f == i1, m1 / tot, jnp.where(lane_f == i2, m2 / tot, 0.0))
    sel_ref[0] = jnp.where((lane_f == i1) | (lane_f == i2), 1.0, 0.0)


def _prep_moe(x, g, shift, scale, w_router_pad, n_exp, h_dtype):
    Bx, R, D = x.shape
    tr = _div_tile(R, 256)
    est = 2 * (_nbytes((tr, D), F32) * 4 + _nbytes((D, LANE), F32))
    return pl.pallas_call(
        functools.partial(_prep_moe_body, n_exp),
        grid=(Bx, R // tr),
        in_specs=[pl.BlockSpec((1, tr, D), lambda b, r: (b, r, 0)),
                  pl.BlockSpec((1, D), lambda b, r: (0, 0)),
                  _mod_spec(shift, R, tr), _mod_spec(scale, R, tr),
                  pl.BlockSpec((D, LANE), lambda b, r: (0, 0))],
        out_specs=[pl.BlockSpec((1, tr, D), lambda b, r: (b, r, 0)),
                   pl.BlockSpec((1, tr, LANE), lambda b, r: (b, r, 0)),
                   pl.BlockSpec((1, tr, LANE), lambda b, r: (b, r, 0))],
        out_shape=[jax.ShapeDtypeStruct((Bx, R, D), h_dtype),
                   jax.ShapeDtypeStruct((Bx, R, LANE), F32),
                   jax.ShapeDtypeStruct((Bx, R, LANE), F32)],
        compiler_params=_params(("parallel", "parallel"), est),
        name="prep_moe",
    )(x, g, shift, scale, w_router_pad)


def _prep_rwkv_body(seq, x_ref, g_ref, sh_ref, sc_ref, mu_ref, sp_ref, lerp_ref, hl_ref, carry_ref):
    h = _rms_mod(x_ref[0], g_ref[...], sh_ref[0], sc_ref[0])
    tr = h.shape[0]
    if seq:
        @pl.when(pl.program_id(1) == 0)
        def _():
            carry_ref[...] = sp_ref[0]
        row = lax.broadcasted_iota(jnp.int32, h.shape, 0)
        h_prev = jnp.where(row == 0, carry_ref[...], pltpu.roll(h, 1, 0))
        carry_ref[...] = h[tr - 1:tr, :]
        hl_ref[0] = h[tr - 1:tr, :]
    else:
        h_prev = sp_ref[0]
        hl_ref[0] = h
    xx = h_prev - h
    for i in range(N_LERP):
        lerp_ref[i, 0] = (h + xx * mu_ref[i:i + 1, :]).astype(lerp_ref.dtype)


def _prep_rwkv(x, g, shift, scale, mu, shift_prev, seq):
    Bx, R, D = x.shape
    tr = _div_tile(R, 128)
    est = 2 * (_nbytes((tr, D), F32) * 4 + _nbytes((N_LERP, tr, D), BF16)) + _nbytes((N_LERP, D), F32)
    if seq:
        sp_spec = pl.BlockSpec((1, 1, D), lambda b, r: (b, 0, 0))
        hl_spec = pl.BlockSpec((1, 1, D), lambda b, r: (b, 0, 0))
        hl_shape = jax.ShapeDtypeStruct((Bx, 1, D), F32)
    else:
        sp_spec = pl.BlockSpec((1, tr, D), lambda b, r: (b, r, 0))
        hl_spec = pl.BlockSpec((1, tr, D), lambda b, r: (b, r, 0))
        hl_shape = jax.ShapeDtypeStruct((Bx, R, D), F32)
    return pl.pallas_call(
        functools.partial(_prep_rwkv_body, seq),
        grid=(Bx, R // tr),
        in_specs=[pl.BlockSpec((1, tr, D), lambda b, r: (b, r, 0)),
                  pl.BlockSpec((1, D), lambda b, r: (0, 0)),
                  _mod_spec(shift, R, tr), _mod_spec(scale, R, tr),
                  pl.BlockSpec((N_LERP, D), lambda b, r: (0, 0)),
                  sp_spec],
        out_specs=[pl.BlockSpec((N_LERP, 1, tr, D), lambda b, r: (0, b, r, 0)), hl_spec],
        out_shape=[jax.ShapeDtypeStruct((N_LERP, Bx, R, D), BF16), hl_shape],
        scratch_shapes=[pltpu.VMEM((1, D), F32)],
        compiler_params=_params(("parallel", "arbitrary"), est),
        name="prep_rwkv",
    )(x, g, shift, scale, mu, shift_prev)


def _linear_body(n_w, n_ex, n_pf, n_into, epilogue, is_padding, *refs):
    pf_refs, refs = refs[:n_pf], refs[n_pf:]
    a_ref = refs[0]
    w_refs = refs[1:1 + n_w]
    ex_refs = refs[1 + n_w:1 + n_w + n_ex]
    out_refs = refs[1 + n_w + n_ex + n_into:]

    def compute():
        a = a_ref[...].reshape(a_ref.shape[-2:])
        accs = [_dot(a, w[...].reshape(w.shape[-2:])) for w in w_refs]
        exs = [e[...].reshape(e.shape[-2:]) for e in ex_refs]
        res = epilogue(*accs, *exs)
        if not isinstance(res, (tuple, list)):
            res = (res,)
        for o, v in zip(out_refs, res):
            o[...] = v.astype(o.dtype).reshape(o.shape)

    if is_padding is None:
        compute()
    else:
        pad = is_padding(*pf_refs)
        pl.when(jnp.logical_not(pad))(compute)

        @pl.when(pad)
        def _():
            for o in out_refs:
                o[...] = jnp.zeros(o.shape, o.dtype)


def _linear(grid, a, a_spec, ws, w_spec, extras, outs, epilogue, sem, prefetch=None, into=None,
            is_padding=None, name="linear"):
    def blk_bytes(spec, dtype):
        dims = [d.block_size if isinstance(d, pl.Element) else (1 if d is None else d)
                for d in spec.block_shape]
        return _nbytes(dims, dtype)

    est = 2 * blk_bytes(a_spec, a.dtype)
    est += sum(2 * blk_bytes(w_spec, w.dtype) + blk_bytes(w_spec, BF16) for w in ws)
    est += sum(2 * blk_bytes(s, e.dtype) for e, s in extras)
    est += sum(4 * blk_bytes(s, F32) for o, s in outs)
    n_pf = 0 if prefetch is None else 1
    body = functools.partial(_linear_body, len(ws), len(extras), n_pf, 0 if into is None else 1, epilogue,
                             is_padding)
    in_specs = [a_spec] + [w_spec] * len(ws) + [s for _, s in extras]
    out_specs = [s for _, s in outs]
    out_shape = [o for o, _ in outs]
    args = [a] + list(ws) + [e for e, _ in extras]
    aliases = {}
    if into is not None:
        aliases = {n_pf + len(args): 0}
        in_specs.append(pl.BlockSpec(memory_space=pl.ANY))
        args.append(into)
    if prefetch is None:
        call = pl.pallas_call(body, grid=grid, in_specs=in_specs, out_specs=out_specs,
                              out_shape=out_shape, input_output_aliases=aliases,
                              compiler_params=_params(sem, est), name=name)
        res = call(*args)
    else:
        gs = pltpu.PrefetchScalarGridSpec(num_scalar_prefetch=1, grid=grid, in_specs=in_specs,
                                          out_specs=out_specs)
        call = pl.pallas_call(body, grid_spec=gs, out_shape=out_shape, input_output_aliases=aliases,
                              compiler_params=_params(sem, est), name=name)
        res = call(prefetch, *args)
    return res


def _row_tile(M):
    return _div_tile(M, 1024)


def _gate_spec_ij(gate, rows_per_batch, tm, tn):
    if gate.shape[1] == 1:
        return pl.BlockSpec((1, 1, tn), lambda i, j: ((i * tm) // rows_per_batch, 0, j))
    assert gate.shape[0] == 1 and gate.shape[1] == tm
    return pl.BlockSpec((1, tm, tn), lambda i, j: (0, 0, j))


def _proj(a, w, w_lead, *, a_lead=0, n_cols=None, epilogue=None, extras=(), out_dtype=F32, tn_pref=512,
          name="proj"):
    _, M, K = a.shape
    N = w.shape[2] if n_cols is None else n_cols
    tm = _row_tile(M)
    tn = _div_tile(N, tn_pref if M > 64 else 1024)
    ex = []
    for e in extras:
        if e.shape[0] == 1:
            ex.append((e, pl.BlockSpec((1, tn), lambda i, j: (0, j))))
        else:
            ex.append((e, pl.BlockSpec((tm, tn), lambda i, j: (i, j))))
    epi = epilogue if epilogue is not None else (lambda acc: acc)
    (out,) = _linear(
        (M // tm, N // tn), a,
        pl.BlockSpec((1, tm, K), lambda i, j: (a_lead, i, 0)),
        [w], pl.BlockSpec((1, K, tn), lambda i, j: (w_lead, 0, j)),
        ex,
        [(jax.ShapeDtypeStruct((M, N), out_dtype), pl.BlockSpec((tm, tn), lambda i, j: (i, j)))],
        epi, ("parallel", "arbitrary"), name=name)
    return out


def _proj_residual(a, w, w_lead, x, gate, rows_per_batch, *, k_blk=0, k_size=None, prev=None,
                   final=True, name="proj_res"):
    M = a.shape[1]
    K = a.shape[2] if k_size is None else k_size
    N = w.shape[2]
    tm = _row_tile(min(M, rows_per_batch) if rows_per_batch > 1 else M)
    tn = _div_tile(N, 256 if M > 64 else 512)
    tile = lambda arr: (arr, pl.BlockSpec((tm, tn), lambda i, j: (i, j)))
    ex = []
    if prev is not None:
        ex.append(tile(prev))
    if final:
        ex.append(tile(x))
        ex.append((gate, _gate_spec_ij(gate, rows_per_batch, tm, tn)))

    def epi(acc, *e):
        e = list(e)
        if prev is not None:
            acc = acc + e.pop(0)
        if final:
            xv, gv = e
            acc = xv + gv * acc
        return acc

    (out,) = _linear(
        (M // tm, N // tn), a,
        pl.BlockSpec((1, tm, K), lambda i, j: (0, i, k_blk)),
        [w], pl.BlockSpec((1, K, tn), lambda i, j: (w_lead, k_blk, j)),
        ex,
        [(jax.ShapeDtypeStruct((M, N), F32), pl.BlockSpec((tm, tn), lambda i, j: (i, j)))],
        epi, ("parallel", "arbitrary"), name=name)
    return out


def _scan_body(C, n_par, n_chunks, has_vres, *refs):
    if has_vres:
        (r_ref, k_ref, v_ref, w_ref, a_ref, g_ref, vg_ref, vf_ref,
         kk_ref, ka_ref, rk_ref, lnw_ref, lnb_ref, s0_ref, z_ref, st_ref, s_scr) = refs
    else:
        (r_ref, k_ref, v_ref, w_ref, a_ref, g_ref,
         kk_ref, ka_ref, rk_ref, lnw_ref, lnb_ref, s0_ref, z_ref, st_ref, s_scr) = refs
    t_id = pl.program_id(2)

    @pl.when(t_id == 0)
    def _():
        s_scr[...] = s0_ref[0]

    half = LANE // 2
    lane = lax.broadcasted_iota(jnp.int32, (1, LANE), 1)
    m0 = lane < half
    r2 = lax.broadcasted_iota(jnp.int32, (2 * C, 2 * C), 0) % C
    c2 = lax.broadcasted_iota(jnp.int32, (2 * C, 2 * C), 1) % C
    stril = r2 > c2
    tril = r2 >= c2
    tri_c = (lax.broadcasted_iota(jnp.int32, (C, C), 0)
             >= lax.broadcasted_iota(jnp.int32, (C, C), 1)).astype(F32)
    n_levels = max(1, int(math.log2(C)))
    inv_half = 1.0 / half

    def seg_sum(x):
        s0 = jnp.sum(jnp.where(m0, x, 0.0), axis=-1, keepdims=True)
        s1 = jnp.sum(jnp.where(m0, 0.0, x), axis=-1, keepdims=True)
        return jnp.where(m0, s0, s1)

    def stack(x):
        return jnp.concatenate([jnp.where(m0, x, 0.0), jnp.where(m0, 0.0, x)], axis=0)

    def each(f, *lists):
        return [f(*xs) for xs in zip(*lists)]

    lss = [slice(i * LANE, (i + 1) * LANE) for i in range(n_par)]

    def chunk(c, states):
        sl = pl.ds(pl.multiple_of(c * C, C), C)
        S = list(states)
        r = [r_ref[0, 0, sl, ls] for ls in lss]
        k = [k_ref[0, 0, sl, ls] for ls in lss]
        v = [v_ref[0, 0, sl, ls] for ls in lss]
        a = [a_ref[0, sl, ls] for ls in lss]
        logd = [-jnp.exp(-_softplus(-w_ref[0, sl, ls]) - 0.5) for ls in lss]
        if has_vres:
            v = [vi + (vf_ref[0, sl, ls] - vi) * vg_ref[0, sl, ls] for vi, ls in zip(v, lss)]
        cum = each(lambda x: jnp.dot(tri_c, x, precision=HI, preferred_element_type=F32), logd)
        kkr = [ki * kk_ref[:, ls] for ki, ls in zip(k, lss)]
        kk = each(lambda x: x / jnp.maximum(jnp.sqrt(seg_sum(x * x)), 1e-12), kkr)
        k2 = [ki * (1.0 + (ai - 1.0) * ka_ref[:, ls]) for ki, ai, ls in zip(k, a, lss)]
        eg = each(jnp.exp, cum)
        eng = each(lambda x: jnp.exp(-x), cum)
        a_st = each(lambda kki, ci, li: stack(-kki * jnp.exp(ci - li)), kk, cum, logd)
        r_st = each(lambda ri, e: stack(ri * e), r, eg)
        b_st = each(lambda kki, ai, e: stack(kki * ai * e), kk, a, eng)
        k_st = each(lambda ki, e: stack(ki * e), k2, eng)
        v_st = each(stack, v)
        ar = each(lambda x, y: jnp.concatenate([x, y], axis=0), a_st, r_st)
        xb = each(_dot_nt, ar, b_st)
        xk = each(_dot_nt, ar, k_st)
        p = each(lambda x: jnp.where(stril, x[:2 * C], 0.0), xb)
        lak = each(lambda x: jnp.where(stril, x[:2 * C], 0.0), xk)
        mrb = each(lambda x: jnp.where(tril, x[2 * C:], 0.0), xb)
        mrk = each(lambda x: jnp.where(tril, x[2 * C:], 0.0), xk)
        q = p
        for _ in range(1, n_levels):
            p = each(_dot, p, p)
            q = each(lambda qi, pi: qi + pi + _dot(qi, pi), q, p)
        lv = each(_dot, lak, v_st)
        mv = each(_dot, mrk, v_st)
        a_s = each(_dot_nt, ar, S)
        wm = each(lambda x, y: x[:2 * C] + y, a_s, lv)
        u = each(lambda qi, w: w + _dot(qi, w), q, wm)
        y_st = each(lambda x, m, ui, mvi: x[2 * C:] + _dot(m, ui) + mvi, a_s, mrb, u, mv)
        y = each(lambda x: x[:C] + x[C:], y_st)
        eg_last = each(lambda e: e[C - 1:C, :], eg)
        S_new = each(lambda Si, e, ui, b, vs, ks: Si * e + _dot_tn(ui, b * e) + _dot_tn(vs, ks * e),
                     S, eg_last, u, b_st, v_st, k_st)
        mean = each(lambda x: seg_sum(x) * inv_half, y)
        yc = each(lambda x, m: x - m, y, mean)
        var = each(lambda x: seg_sum(x * x) * inv_half, yc)
        for i, ls in enumerate(lss):
            yn = yc[i] * lax.rsqrt(var[i] + GN_EPS) * lnw_ref[:, ls] + lnb_ref[:, ls]
            bonus = seg_sum(r[i] * k2[i] * rk_ref[:, ls]) * v[i]
            z_ref[0, sl, ls] = ((yn + bonus) * g_ref[0, sl, ls]).astype(z_ref.dtype)
        return tuple(S_new)

    states = lax.fori_loop(0, n_chunks, chunk, tuple(s_scr[i] for i in range(n_par)))
    for i in range(n_par):
        s_scr[i] = states[i]

    @pl.when(t_id == pl.num_programs(2) - 1)
    def _():
        st_ref[0] = s_scr[...]


def _rwkv_scan(rkv, w_pre, a_gate, g, vres, vecs, s0_bd, C):
    _, B, T, D = rkv.shape
    n_par = SCAN_PAIRS if (D // LANE) % SCAN_PAIRS == 0 else 1
    W = n_par * LANE
    nb = D // W
    tb = _div_tile(T, SCAN_ROWS)
    assert tb % C == 0
    seq_spec = pl.BlockSpec((1, tb, W), lambda b, h, t: (b, t, h))
    rkv_spec = lambda i: pl.BlockSpec((1, 1, tb, W), lambda b, h, t: (i, b, t, h))
    vec_spec = pl.BlockSpec((1, W), lambda b, h, t: (0, h))
    st_spec = pl.BlockSpec((1, n_par, LANE, LANE), lambda b, h, t: (b, h, 0, 0))
    has_vres = vres is not None
    n_seq = 6 + (2 if has_vres else 0)
    in_specs = [rkv_spec(0), rkv_spec(1), rkv_spec(2)] + [seq_spec] * (n_seq - 3) + [vec_spec] * 5 + [st_spec]
    args = [rkv, rkv, rkv, w_pre, a_gate, g] + (list(vres) if has_vres else []) + list(vecs) + [s0_bd]
    est = 2 * (n_seq * _nbytes((tb, W), F32) + _nbytes((tb, W), BF16) + 2 * _nbytes((n_par, LANE, LANE), F32))
    est += n_par * 64 * _nbytes((2 * C, LANE), F32)
    return pl.pallas_call(
        functools.partial(_scan_body, C, n_par, tb // C, has_vres),
        grid=(B, nb, T // tb),
        in_specs=in_specs,
        out_specs=[seq_spec, st_spec],
        out_shape=[jax.ShapeDtypeStruct((B, T, D), BF16),
                   jax.ShapeDtypeStruct((B, D // LANE, LANE, LANE), F32)],
        scratch_shapes=[pltpu.VMEM((n_par, LANE, LANE), F32)],
        compiler_params=_params(("parallel", "parallel", "arbitrary"), est),
        name="rwkv_scan",
    )(*args)


def _to_block_diag(s):
    B, H, N, _ = s.shape
    s = s.reshape(B, H // 2, 2, N, N)
    z = jnp.zeros_like(s[:, :, 0])
    top = jnp.concatenate([s[:, :, 0], z], axis=-1)
    bot = jnp.concatenate([z, s[:, :, 1]], axis=-1)
    return jnp.concatenate([top, bot], axis=-2)


def _from_block_diag(s_bd, N):
    B, nb = s_bd.shape[:2]
    return jnp.stack([s_bd[:, :, :N, :N], s_bd[:, :, N:, N:]], axis=2).reshape(B, 2 * nb, N, N)


def _rwkv_step_body(has_vres, mxu_round, *refs):
    if has_vres:
        (r_ref, k_ref, v_ref, w_ref, a_ref, g_ref, vg_ref, vf_ref,
         kk_ref, ka_ref, rk_ref, lnw_ref, lnb_ref, s0_ref, z_ref, st_ref) = refs
    else:
        (r_ref, k_ref, v_ref, w_ref, a_ref, g_ref,
         kk_ref, ka_ref, rk_ref, lnw_ref, lnb_ref, s0_ref, z_ref, st_ref) = refs
    r, k, v, a = r_ref[0], k_ref[0], v_ref[0], a_ref[0]
    S = s0_ref[0]
    N = S.shape[-1]
    eye = (lax.broadcasted_iota(jnp.int32, (N, N), 0) == lax.broadcasted_iota(jnp.int32, (N, N), 1))
    col = lambda x: jnp.sum(jnp.where(eye, x, 0.0), axis=-1, keepdims=True)
    row = lambda x: jnp.sum(jnp.where(eye, x, 0.0), axis=-2, keepdims=True)
    rnd = (lambda x: x.astype(BF16).astype(F32)) if mxu_round else (lambda x: x)
    decay = jnp.exp(-jnp.exp(-_softplus(-w_ref[0]) - 0.5))
    if has_vres:
        v = v + (vf_ref[0] - v) * vg_ref[0]
    kkr = k * kk_ref[...]
    kk = kkr / jnp.maximum(jnp.sqrt(jnp.sum(kkr * kkr, axis=-1, keepdims=True)), 1e-12)
    k2 = k * (1.0 + (a - 1.0) * ka_ref[...])
    sa = jnp.sum(rnd(S) * rnd(-kk), axis=-1, keepdims=True)
    S_new = S * decay + sa * (kk * a) + col(v) * k2
    st_ref[0] = S_new
    y = jnp.sum(rnd(S_new) * rnd(r), axis=-1, keepdims=True)
    mean = jnp.mean(y, axis=-2, keepdims=True)
    yc = y - mean
    var = jnp.mean(yc * yc, axis=-2, keepdims=True)
    yn = row(yc * lax.rsqrt(var + GN_EPS)) * lnw_ref[...] + lnb_ref[...]
    bonus = jnp.sum(r * k2 * rk_ref[...], axis=-1, keepdims=True) * v
    z_ref[0] = (yn + bonus) * g_ref[0]


def _rwkv_step(seqs, vecs, s0, mxu_round):
    Bd, H, N, _ = s0.shape
    row_spec = pl.BlockSpec((1, H, 1, N), lambda b: (b, 0, 0, 0))
    vec_spec = pl.BlockSpec((H, 1, N), lambda b: (0, 0, 0))
    st_spec = pl.BlockSpec((1, H, N, N), lambda b: (b, 0, 0, 0))
    est = 24 * _nbytes((H, N, LANE), F32)
    return pl.pallas_call(
        functools.partial(_rwkv_step_body, len(seqs) == 8, mxu_round),
        grid=(Bd,),
        in_specs=[row_spec] * len(seqs) + [vec_spec] * 5 + [st_spec],
        out_specs=[row_spec, st_spec],
        out_shape=[jax.ShapeDtypeStruct((Bd, H, 1, N), F32), jax.ShapeDtypeStruct((Bd, H, N, N), F32)],
        compiler_params=_params(("parallel",), est),
        name="rwkv_step",
    )(*seqs, *vecs, s0)


def _cumsum_body(tb, lf_ref, f_ref):
    T = lf_ref.shape[1]
    tri = (lax.broadcasted_iota(jnp.int32, (tb, tb), 0)
           >= lax.broadcasted_iota(jnp.int32, (tb, tb), 1)).astype(F32)

    def blk(i, carry):
        sl = pl.ds(pl.multiple_of(i * tb, tb), tb)
        f = jnp.dot(tri, lf_ref[0, sl, :], precision=HI, preferred_element_type=F32) + carry
        f_ref[0, sl, :] = f
        return f[tb - 1:tb, :]

    lax.fori_loop(0, T // tb, blk, jnp.zeros((1, lf_ref.shape[2]), F32))


def _cumsum_rows(lf):
    B, T, W = lf.shape
    tb = _div_tile(T, 256)
    return pl.pallas_call(
        functools.partial(_cumsum_body, tb),
        grid=(B,),
        in_specs=[pl.BlockSpec((1, T, W), lambda b: (b, 0, 0))],
        out_specs=pl.BlockSpec((1, T, W), lambda b: (b, 0, 0)),
        out_shape=jax.ShapeDtypeStruct((B, T, W), F32),
        compiler_params=_params(("parallel",), 4 * _nbytes((T, W), F32)),
        name="logf_cumsum",
    )(lf)


def _fox_prefill_body(tq, n_h, scale, q_ref, k_ref, v_ref, fr_ref, fc_ref, o_ref, kb_ref, vb_ref):
    T = q_ref.shape[2]
    kb_ref[...] = k_ref[0, 0].astype(BF16)
    vb_ref[...] = v_ref[0, 0].astype(BF16)
    causal = (lax.broadcasted_iota(jnp.int32, (tq, tq), 1)
              <= lax.broadcasted_iota(jnp.int32, (tq, tq), 0))
    lss = [slice(i * LANE, (i + 1) * LANE) for i in range(n_h)]
    heads = range(n_h)

    def q_blk(qi, _):
        qs = pl.ds(pl.multiple_of(qi * tq, tq), tq)
        q = [q_ref[0, 0, qs, ls].astype(BF16) for ls in lss]
        fq = [fc_ref[0, i, qs, :] for i in heads]

        def kv_blk(ks, carry, diagonal):
            m, l, acc = carry
            s = [_dot_nt(q[i], kb_ref[ks, lss[i]]) * scale + fq[i] - fr_ref[0, i, :, ks] for i in heads]
            if diagonal:
                s = [jnp.where(causal, x, NEG_INF) for x in s]
            m_new = [jnp.maximum(m[i], jnp.max(s[i], axis=-1, keepdims=True)) for i in heads]
            alpha = [jnp.exp(m[i] - m_new[i]) for i in heads]
            p = [jnp.exp(s[i] - m_new[i]) for i in heads]
            l = [alpha[i] * l[i] + jnp.sum(p[i], axis=-1, keepdims=True) for i in heads]
            acc = [alpha[i] * acc[i] + _dot(p[i], vb_ref[ks, lss[i]]) for i in heads]
            return tuple(m_new), tuple(l), tuple(acc)

        init = (tuple(jnp.full((tq, 1), NEG_INF, F32) for _ in heads),
                tuple(jnp.zeros((tq, 1), F32) for _ in heads),
                tuple(jnp.zeros((tq, LANE), F32) for _ in heads))
        carry = lax.fori_loop(
            0, qi, lambda kj, c: kv_blk(pl.ds(pl.multiple_of(kj * tq, tq), tq), c, False), init)
        _, l, acc = kv_blk(qs, carry, True)
        for i in heads:
            o_ref[0, qs, lss[i]] = (acc[i] / l[i]).astype(o_ref.dtype)
        return 0

    lax.fori_loop(0, T // tq, q_blk, 0)


def _fox_prefill(qk, v, f_row, f_col, head_dim):
    _, B, T, D = qk.shape
    H = D // head_dim
    assert head_dim == LANE
    tq = _div_tile(T, 512)
    n_h = PREFILL_HEADS if H % PREFILL_HEADS == 0 else 1
    W = n_h * LANE
    spec = lambda i: pl.BlockSpec((1, 1, T, W), lambda b, h: (i, b, 0, h))
    est = 2 * (4 * _nbytes((T, W), F32) + n_h * _nbytes((T, LANE), F32) + 2 * _nbytes((T, W), BF16))
    est += 6 * n_h * _nbytes((tq, tq), F32)
    return pl.pallas_call(
        functools.partial(_fox_prefill_body, tq, n_h, head_dim ** -0.5),
        grid=(B, H // n_h),
        in_specs=[spec(0), spec(1), spec(0),
                  pl.BlockSpec((1, n_h, 1, T), lambda b, h: (b, h, 0, 0)),
                  pl.BlockSpec((1, n_h, T, 1), lambda b, h: (b, h, 0, 0))],
        out_specs=pl.BlockSpec((1, T, W), lambda b, h: (b, 0, h)),
        out_shape=jax.ShapeDtypeStruct((B, T, D), BF16),
        scratch_shapes=[pltpu.VMEM((T, W), BF16), pltpu.VMEM((T, W), BF16)],
        compiler_params=_params(("parallel", "parallel"), est),
        name="fox_prefill",
    )(qk, qk, v, f_row, f_col)


def _fox_decode_body(scale, pt_ref, q_ref, kn_ref, vn_ref, lfn_ref, kp_ref, vp_ref, lfp_ref,
                     o_ref, m_ref, l_ref, acc_ref, carry_ref):
    p_id = pl.program_id(1)
    n_pages = pl.num_programs(1)
    P, n_grp = kp_ref.shape[0], kp_ref.shape[1]
    W = SUBLANE * P

    @pl.when(p_id == 0)
    def _():
        m_ref[...] = jnp.full(m_ref.shape, NEG_INF, F32)
        l_ref[...] = jnp.zeros(l_ref.shape, F32)
        acc_ref[...] = jnp.zeros(acc_ref.shape, F32)
        carry_ref[...] = lfn_ref[0]

    lf = lfp_ref[...]
    pos_r = lax.broadcasted_iota(jnp.int32, (P, W + LANE), 0)
    pos_c = lax.broadcasted_iota(jnp.int32, (P, W + LANE), 1)
    sums = lax.dot_general(lf, ((pos_r > pos_c // SUBLANE) | (pos_c >= W)).astype(F32),
                           (((0,), (0,)), ((), ())), precision=HI, preferred_element_type=F32)
    bias = sums[:, :W] + carry_ref[...]
    carry_ref[...] = carry_ref[...] + sums[:, W:W + 1]

    own = (lax.broadcasted_iota(jnp.int32, (SUBLANE, W), 1) % SUBLANE
           == lax.broadcasted_iota(jnp.int32, (SUBLANE, W), 0))
    q = q_ref[0]
    s_grp = []
    for g in range(n_grp):
        rows = slice(g * SUBLANE, (g + 1) * SUBLANE)
        sg = _dot_nt(q[rows], kp_ref[:, g].reshape(W, LANE))
        s_grp.append(jnp.where(own, sg * scale + bias[rows], NEG_INF))
    s = jnp.concatenate(s_grp, axis=0)
    m_new = jnp.maximum(m_ref[...], jnp.max(s, axis=-1, keepdims=True))
    alpha = jnp.exp(m_ref[...] - m_new)
    p = jnp.exp(s - m_new)
    l_ref[...] = alpha * l_ref[...] + jnp.sum(p, axis=-1, keepdims=True)
    m_ref[...] = m_new
    pv = [_dot(p[g * SUBLANE:(g + 1) * SUBLANE], vp_ref[:, g].reshape(W, LANE)) for g in range(n_grp)]
    acc_ref[...] = alpha * acc_ref[...] + jnp.concatenate(pv, axis=0)

    @pl.when(p_id == n_pages - 1)
    def _():
        s_new = jnp.sum(q_ref[0] * kn_ref[0], axis=-1, keepdims=True) * scale
        m_fin = jnp.maximum(m_ref[...], s_new)
        al = jnp.exp(m_ref[...] - m_fin)
        p_new = jnp.exp(s_new - m_fin)
        l_fin = al * l_ref[...] + p_new
        o_ref[0] = ((al * acc_ref[...] + p_new * vn_ref[0]) / l_fin).astype(o_ref.dtype)


def _fox_decode(q, k_new, v_new, lf_new, cache_k, cache_v, cache_lf, layer, page_table):
    Bd, H, hd = q.shape
    assert hd == LANE and H % SUBLANE == 0
    P = cache_lf.shape[2]
    G = H // SUBLANE
    n_pages = page_table.shape[1]
    row = lambda: pl.BlockSpec((1, H, hd), lambda b, p, pt: (b, 0, 0))
    page_idx = lambda b, p, pt: (layer, pt[b, n_pages - 1 - p], 0, 0, 0, 0)
    gs = pltpu.PrefetchScalarGridSpec(
        num_scalar_prefetch=1,
        grid=(Bd, n_pages),
        in_specs=[row(), row(), row(),
                  pl.BlockSpec((1, H, 1), lambda b, p, pt: (b, 0, 0)),
                  pl.BlockSpec((None, None, P, G, SUBLANE, hd), page_idx),
                  pl.BlockSpec((None, None, P, G, SUBLANE, hd), page_idx),
                  pl.BlockSpec((None, None, P, H), lambda b, p, pt: (layer, pt[b, n_pages - 1 - p], 0, 0))],
        out_specs=pl.BlockSpec((1, H, hd), lambda b, p, pt: (b, 0, 0)),
        scratch_shapes=[pltpu.VMEM((H, 1), F32), pltpu.VMEM((H, 1), F32),
                        pltpu.VMEM((H, LANE), F32), pltpu.VMEM((H, 1), F32)])
    est = 4 * _nbytes((P * H, hd), F32) + 8 * _nbytes((H, SUBLANE * P), F32) + 2 * MIB
    return pl.pallas_call(
        functools.partial(_fox_decode_body, hd ** -0.5),
        grid_spec=gs,
        out_shape=jax.ShapeDtypeStruct((Bd, H, hd), BF16),
        compiler_params=_params(("parallel", "arbitrary"), est),
        name="fox_decode",
    )(page_table, q, k_new, v_new, lf_new, cache_k, cache_v, cache_lf)


def _head_rms(acc, g_row, head_dim):
    tm, tn = acc.shape
    parts = []
    for h in range(tn // head_dim):
        blk = acc[:, h * head_dim:(h + 1) * head_dim]
        blk = blk * lax.rsqrt(jnp.mean(blk * blk, axis=-1, keepdims=True) + NORM_EPS)
        parts.append(blk * g_row)
    return parts[0] if len(parts) == 1 else jnp.concatenate(parts, axis=-1)


def _qkv_proj(h, w_qkvf, layer, qk_gain, head_dim):
    _, M, D = h.shape
    tm = _row_tile(M)
    tn = _div_tile(D, 512 if M > 64 else 1024)
    nj = D // tn

    def epi(acc, gq):
        return _head_rms(acc, gq, head_dim)

    (qk,) = _linear(
        (M // tm, 2 * nj), h,
        pl.BlockSpec((1, tm, D), lambda i, j: (0, i, 0)),
        [w_qkvf], pl.BlockSpec((1, D, tn), lambda i, j: (layer, 0, j)),
        [(qk_gain, pl.BlockSpec((1, 1, head_dim), lambda i, j: (j // nj, 0, 0)))],
        [(jax.ShapeDtypeStruct((2, M, D), F32),
          pl.BlockSpec((1, tm, tn), lambda i, j: (j // nj, i, j % nj)))],
        epi, ("parallel", "arbitrary"), name="qk_proj")
    (v,) = _linear(
        (M // tm, nj), h,
        pl.BlockSpec((1, tm, D), lambda i, j: (0, i, 0)),
        [w_qkvf], pl.BlockSpec((1, D, tn), lambda i, j: (layer, 0, 2 * nj + j)),
        [],
        [(jax.ShapeDtypeStruct((1, M, D), F32), pl.BlockSpec((1, tm, tn), lambda i, j: (0, i, j)))],
        lambda acc: acc, ("parallel", "arbitrary"), name="v_proj")
    return qk, v


def _swiglu_epi(g, u):
    return (g * jax.nn.sigmoid(g)) * u


def _col_segments(F, tn):
    assert F % LANE == 0 and tn % LANE == 0
    n = F // tn
    segs = [(0, tn, n)] if n else []
    if F - n * tn:
        segs.append((n * tn, F - n * tn, 1))
    return segs


def _ffn_up_dense(h, w_gate, w_up, layer):
    _, M, D = h.shape
    F = w_gate.shape[2]
    tm = _row_tile(M)
    E = pl.Element
    segs = _col_segments(F, 256 if M > 64 else 512)
    out = jnp.zeros((M, F), BF16) if len(segs) > 1 else None
    for c0, tn, n_j in segs:
        off = lambda j: pl.multiple_of(c0 + j * tn, LANE)
        (out,) = _linear(
            (M // tm, n_j), h,
            pl.BlockSpec((1, tm, D), lambda i, j: (0, i, 0)),
            [w_gate, w_up], pl.BlockSpec((None, E(D), E(tn)), lambda i, j: (layer, 0, off(j))),
            [],
            [(jax.ShapeDtypeStruct((M, F), BF16),
              pl.BlockSpec((E(tm), E(tn)), lambda i, j: (i * tm, off(j))))],
            _swiglu_epi, ("parallel", "arbitrary"), into=out, name="ffn_up")
    return out


def _moe_up_sorted(a_sorted, w_gate, w_up, layer_base, n_exp, tile_expert, tm):
    _, Mp, D = a_sorted.shape
    F = w_gate.shape[2]
    E = pl.Element
    segs = _col_segments(F, 512)
    out = jnp.zeros((Mp, F), BF16) if len(segs) > 1 else None
    for c0, tn, n_j in segs:
        off = lambda j: pl.multiple_of(c0 + j * tn, LANE)
        (out,) = _linear(
            (n_j, Mp // tm), a_sorted,
            pl.BlockSpec((1, tm, D), lambda j, s, te: (0, s, 0)),
            [w_gate, w_up],
            pl.BlockSpec((None, E(D), E(tn)), lambda j, s, te: (layer_base + te[s] % n_exp, 0, off(j))),
            [],
            [(jax.ShapeDtypeStruct((Mp, F), BF16),
              pl.BlockSpec((E(tm), E(tn)), lambda j, s, te: (s * tm, off(j))))],
            _swiglu_epi, ("parallel", "arbitrary"), prefetch=tile_expert, into=out,
            is_padding=lambda te: te[pl.program_id(1)] >= n_exp, name="moe_up")
    return out


def _moe_down_sorted(a_sorted, w_down, layer_base, n_exp, tile_expert, pair_gate, tm):
    _, Mp, F = a_sorted.shape
    D = w_down.shape[2]
    tn = _div_tile(D, 512)
    (out,) = _linear(
        (D // tn, Mp // tm), a_sorted,
        pl.BlockSpec((1, tm, F), lambda j, s, te: (0, s, 0)),
        [w_down], pl.BlockSpec((1, F, tn), lambda j, s, te: (layer_base + te[s] % n_exp, 0, j)),
        [(pair_gate, pl.BlockSpec((tm, 1), lambda j, s, te: (s, 0)))],
        [(jax.ShapeDtypeStruct((Mp, D), F32), pl.BlockSpec((tm, tn), lambda j, s, te: (s, j)))],
        lambda acc, pg: acc * pg, ("parallel", "arbitrary"), prefetch=tile_expert,
        is_padding=lambda te: te[pl.program_id(1)] >= n_exp, name="moe_down")
    return out


def _row_copy(src_ref, row, dst_ref, sem):
    return pltpu.make_async_copy(src_ref.at[pl.ds(row, 1), :], dst_ref, sem)


def _gather_rows_body(n_rows, idx_ref, src_ref, out_ref, buf_ref, sem_ref):
    s = pl.program_id(0)
    slot = s % 2

    def issue(step, to_slot):
        def pair(i, carry):
            for prio in range(2):
                r = 2 * i + prio
                _row_copy(src_ref, idx_ref[step * n_rows + r], buf_ref.at[to_slot, pl.ds(r, 1), :],
                          sem_ref.at[to_slot]).start(priority=prio)
            return carry
        lax.fori_loop(0, n_rows // 2, pair, 0)

    @pl.when(s == 0)
    def _():
        issue(0, 0)

    @pl.when(s + 1 < pl.num_programs(0))
    def _():
        issue(s + 1, 1 - slot)

    def wait_one(r, carry):
        _row_copy(src_ref, 0, buf_ref.at[slot, pl.ds(r, 1), :], sem_ref.at[slot]).wait()
        return carry
    lax.fori_loop(0, n_rows, wait_one, 0)
    out_ref[...] = buf_ref[slot].astype(out_ref.dtype)


def _gather_rows(src, idx, n_rows=256):
    M, D = src.shape
    n_out = idx.shape[0]
    assert n_out % n_rows == 0
    gs = pltpu.PrefetchScalarGridSpec(
        num_scalar_prefetch=1, grid=(n_out // n_rows,),
        in_specs=[pl.BlockSpec(memory_space=pl.ANY)],
        out_specs=pl.BlockSpec((n_rows, D), lambda s, ix: (s, 0)),
        scratch_shapes=[pltpu.VMEM((2, n_rows, D), F32), pltpu.SemaphoreType.DMA((2,))])
    est = 2 * _nbytes((n_rows, D), F32) + 3 * _nbytes((n_rows, D), BF16)
    return pl.pallas_call(
        functools.partial(_gather_rows_body, n_rows),
        grid_spec=gs,
        out_shape=jax.ShapeDtypeStruct((n_out, D), BF16),
        compiler_params=_params(("arbitrary",), est),
        name="moe_gather",
    )(idx, src)


def _combine_body(n_tok, pos_ref, y_ref, x_ref, gate_ref, out_ref, buf_ref, sem_ref):
    n_inner = pl.num_programs(1)
    s = pl.program_id(0) * n_inner + pl.program_id(1)
    n_steps = pl.num_programs(0) * n_inner
    slot = s % 2

    def issue(step, to_slot):
        def one(t, carry):
            for k in range(TOP_K):
                _row_copy(y_ref, pos_ref[(step * n_tok + t) * TOP_K + k],
                          buf_ref.at[to_slot, k, pl.ds(t, 1), :], sem_ref.at[to_slot]).start()
            return carry
        lax.fori_loop(0, n_tok, one, 0)

    @pl.when(s == 0)
    def _():
        issue(0, 0)

    @pl.when(s + 1 < n_steps)
    def _():
        issue(s + 1, 1 - slot)

    def wait_one(t, carry):
        for k in range(TOP_K):
            _row_copy(y_ref, 0, buf_ref.at[slot, k, pl.ds(t, 1), :], sem_ref.at[slot]).wait()
        return carry
    lax.fori_loop(0, n_tok, wait_one, 0)
    acc = buf_ref[slot, 0]
    for k in range(1, TOP_K):
        acc = acc + buf_ref[slot, k]
    out_ref[0] = x_ref[0] + gate_ref[0] * acc


def _moe_combine(y_sorted, pos, x, gate, n_tok=128):
    Bx, R, D = x.shape
    n_tok = _div_tile(R, n_tok)
    if gate.shape[1] == 1:
        gate_spec = pl.BlockSpec((1, 1, D), lambda b, s, ps: (b, 0, 0))
    else:
        gate_spec = pl.BlockSpec((1, n_tok, D), lambda b, s, ps: (b, s, 0))
    gs = pltpu.PrefetchScalarGridSpec(
        num_scalar_prefetch=1, grid=(Bx, R // n_tok),
        in_specs=[pl.BlockSpec(memory_space=pl.ANY),
                  pl.BlockSpec((1, n_tok, D), lambda b, s, ps: (b, s, 0)),
                  gate_spec],
        out_specs=pl.BlockSpec((1, n_tok, D), lambda b, s, ps: (b, s, 0)),
        scratch_shapes=[pltpu.VMEM((2, TOP_K, n_tok, D), F32), pltpu.SemaphoreType.DMA((2,))])
    est = (2 * TOP_K + 6) * _nbytes((n_tok, D), F32)
    return pl.pallas_call(
        functools.partial(_combine_body, n_tok),
        grid_spec=gs,
        out_shape=jax.ShapeDtypeStruct((Bx, R, D), F32),
        compiler_params=_params(("arbitrary", "arbitrary"), est),
        name="moe_combine",
    )(pos, y_sorted, x, gate)


def _route_metadata(gates, selmask, n_exp, tm):
    M = gates.shape[0]
    n_tiles = (M * TOP_K) // tm + n_exp
    sel = selmask > 0.5
    seli = sel.astype(jnp.int32)
    rank = jnp.cumsum(seli, axis=0) - seli
    counts = jnp.sum(seli, axis=0)
    tiles_per = (counts + tm - 1) // tm
    tile_start = jnp.cumsum(tiles_per) - tiles_per
    pos = tile_start[None, :] * tm + rank
    tile_ids = jnp.arange(n_tiles, dtype=jnp.int32)
    tile_expert = jnp.clip(jnp.searchsorted(jnp.cumsum(tiles_per), tile_ids, side="right"),
                           0, n_exp - 1).astype(jnp.int32)
    tile_expert = jnp.where(tile_ids < jnp.sum(tiles_per), tile_expert, tile_expert + n_exp)
    flat_pos = jnp.where(sel, pos, n_tiles * tm).reshape(-1)
    pair = jnp.zeros((n_tiles * tm,), jnp.int32).at[flat_pos].set(
        jnp.arange(1, M * n_exp + 1, dtype=jnp.int32), mode="drop")
    pair_idx = jnp.maximum(pair - 1, 0)
    row_src = pair_idx // n_exp
    row_gate = jnp.where(pair > 0, gates.reshape(-1)[pair_idx], 0.0)
    order = jnp.argsort(jnp.where(sel, 0, 1), axis=1, stable=True)[:, :TOP_K]
    tok_pos = jnp.take_along_axis(pos, order, axis=1).astype(jnp.int32).reshape(-1)
    return row_src, tile_expert, row_gate.reshape(-1, 1), tok_pos


def _trunk(x, mods, rows_per_batch, seq, P, shift_in, wkv_in, paged):
    Bx, R, D = x.shape
    M = Bx * R
    depth = P["norm_g"].shape[0]
    n_rwkv_heads, n_state = P["state_dims"]
    head_dim = P["attn_q_g"].shape[-1]
    n_attn_heads = D // head_dim
    n_exp = P["moe_w_router"].shape[-1]
    lo_w, lo_a, lo_g = P["rwkv_w1"].shape[-1], P["rwkv_a1"].shape[-1], P["rwkv_g1"].shape[-1]
    lo_v = P["rwkv_v1"].shape[-1]
    n_seqs = Bx if seq else R
    T = R if seq else 1
    flat = lambda t: t.reshape(1, M, t.shape[-1])

    shifts, wkvs, ks, vs, lfs = [], [], [], [], []
    v_first = None
    for i in range(depth):
        j = i // 2
        shift, scale, gate = mods[(i, 0)]
        g_row = P["norm_g"][i, 0][None, :]
        if i % 2 == 0:
            mu = P["rwkv_mu"][j][jnp.array([0, 2, 3, 1, 4, 5])]
            if seq:
                sp = jnp.zeros((Bx, 1, D), F32) if shift_in is None else shift_in[j][:, None, :]
            else:
                sp = shift_in[j][None]
            lerps, h_keep = _prep_rwkv(x, g_row, shift, scale, mu, sp, seq)
            lerps = lerps.reshape(N_LERP, M, D)
            shifts.append(h_keep.reshape(n_seqs, D))
            n_l = P["rwkv_w_rkv"].shape[0]
            w_rkv = P["rwkv_w_rkv"].reshape(n_l * 3, D, D)
            tm = _row_tile(M)
            tn = _div_tile(D, 512 if M > 64 else 1024)
            (rkv,) = _linear(
                (3, M // tm, D // tn), lerps,
                pl.BlockSpec((1, tm, D), lambda s, a, b: (s, a, 0)),
                [w_rkv], pl.BlockSpec((1, D, tn), lambda s, a, b: (3 * j + s, 0, b)),
                [],
                [(jax.ShapeDtypeStruct((3, M, D), F32), pl.BlockSpec((1, tm, tn), lambda s, a, b: (s, a, b)))],
                lambda acc: acc, ("parallel", "parallel", "arbitrary"), name="rkv_proj")
            row = lambda name: P[name][j][None, :]
            w_mid = _proj(lerps, P["rwkv_w1"], j, a_lead=3, epilogue=jnp.tanh, out_dtype=BF16, name="lora_w1")
            w_pre = _proj(w_mid[None], P["rwkv_w2"], j, extras=[row("rwkv_w0")],
                          epilogue=lambda acc, b: acc + b, name="lora_w2")
            a_mid = _proj(lerps, P["rwkv_a1"], j, a_lead=4, out_dtype=BF16, name="lora_a1")
            a_gate = _proj(a_mid[None], P["rwkv_a2"], j, extras=[row("rwkv_a0")],
                           epilogue=lambda acc, b: jax.nn.sigmoid(acc + b), name="lora_a2")
            g_mid = _proj(lerps, P["rwkv_g1"], j, a_lead=5, epilogue=jax.nn.sigmoid, out_dtype=BF16,
                          name="lora_g1")
            g_out = _proj(g_mid[None], P["rwkv_g2"], j, name="lora_g2")
            if j == 0:
                vres = None
                v_first = rkv[2]
            else:
                v_mid = _proj(lerps, P["rwkv_v1"], j - 1, a_lead=2, out_dtype=BF16, name="lora_v1")
                v_gate = _proj(v_mid[None], P["rwkv_v2"], j - 1, extras=[P["rwkv_v0"][j - 1][None, :]],
                               epilogue=lambda acc, b: jax.nn.sigmoid(acc + b), name="lora_v2")
                vres = (v_gate, v_first)
            s0 = (jnp.zeros((n_seqs, n_rwkv_heads, n_state, n_state), F32) if wkv_in is None else wkv_in[j])
            vecs = (row("rwkv_k_k"), row("rwkv_k_a"), P["rwkv_r_k"][j].reshape(1, D),
                    row("rwkv_ln_w"), row("rwkv_ln_b"))
            if seq:
                shp = lambda t: t.reshape(n_seqs, T, D)
                z, s_bd = _rwkv_scan(rkv.reshape(3, n_seqs, T, D), shp(w_pre), shp(a_gate), shp(g_out),
                                     None if vres is None else tuple(shp(t) for t in vres),
                                     vecs, _to_block_diag(s0), SCAN_CHUNK)
                wkvs.append(_from_block_diag(s_bd, n_state))
                z = z.reshape(1, M, D)
            else:
                hs = lambda t: t.reshape(n_seqs, n_rwkv_heads, 1, n_state)
                seqs = [hs(rkv[0]), hs(rkv[1]), hs(rkv[2]), hs(w_pre), hs(a_gate), hs(g_out)]
                if vres is not None:
                    seqs += [hs(vres[0]), hs(vres[1])]
                z, s_new = _rwkv_step(seqs, [t.reshape(n_rwkv_heads, 1, n_state) for t in vecs], s0,
                                      STEP_MXU_ROUND)
                wkvs.append(s_new)
                z = z.reshape(1, M, D)
            x = _proj_residual(z, P["rwkv_w_o"], j, x.reshape(M, D), gate, rows_per_batch,
                               name="rwkv_out").reshape(Bx, R, D)
        else:
            h = _prep_plain(x, g_row, shift, scale)
            qk, v_att = _qkv_proj(flat(h), P["attn_w_qkvf"], j,
                                  jnp.stack([P["attn_q_g"][j], P["attn_k_g"][j]])[:, None, :], head_dim)
            w_f = jnp.pad(P["attn_w_qkvf"][j][:, 3 * D:], ((0, 0), (0, LANE - n_attn_heads)))[None]
            b_f = jnp.pad(P["attn_b_f"][j], (0, LANE - n_attn_heads))[None, :]
            logf = _proj(flat(h), w_f, 0, extras=[b_f], epilogue=lambda acc, b: -_softplus(-(acc + b)),
                         name="logf_proj")
            ks.append(qk[1].reshape(n_seqs, T, n_attn_heads, head_dim))
            vs.append(v_att[0].reshape(n_seqs, T, n_attn_heads, head_dim))
            lfs.append(logf[:, :n_attn_heads].reshape(n_seqs, T, n_attn_heads))
            if not paged:
                f = _cumsum_rows(logf.reshape(n_seqs, T, LANE))[:, :, :n_attn_heads]
                f_bht = jnp.transpose(f, (0, 2, 1))
                o = _fox_prefill(qk.reshape(2, n_seqs, T, D), v_att.reshape(1, n_seqs, T, D),
                                 f_bht[:, :, None, :], f_bht[:, :, :, None], head_dim)
            else:
                hv = lambda t: t.reshape(n_seqs, n_attn_heads, head_dim)
                o = _fox_decode(hv(qk[0]), hv(qk[1]), hv(v_att[0]),
                                logf[:, :n_attn_heads].reshape(n_seqs, n_attn_heads, 1),
                                P["cache_k"], P["cache_v"], P["cache_lf"], j, P["page_table"])
            x = _proj_residual(o.reshape(1, M, D), P["attn_w_o"], j, x.reshape(M, D), gate, rows_per_batch,
                               name="attn_out").reshape(Bx, R, D)

        shift, scale, gate = mods[(i, 1)]
        g_row = P["norm_g"][i, 1][None, :]
        if i % 2 == 0:
            h = _prep_plain(x, g_row, shift, scale)
            hid = _ffn_up_dense(flat(h), P["ffn_w_gate"], P["ffn_w_up"], j)
            F = hid.shape[1]
            part = _proj_residual(hid[None], P["ffn_w_down"], j, None, None, rows_per_batch,
                                  k_blk=0, k_size=F // 2, final=False, name="ffn_down0")
            x = _proj_residual(hid[None], P["ffn_w_down"], j, x.reshape(M, D), gate, rows_per_batch,
                               k_blk=1, k_size=F // 2, prev=part, name="ffn_down1").reshape(Bx, R, D)
        else:
            x = yield (x, g_row, shift, scale, gate, j)
    return x, ks, vs, lfs, shifts, wkvs


def _moe_joint(requests, P):
    j = requests[0][5]
    g_row = requests[0][1]
    D = requests[0][0].shape[-1]
    n_exp = P["moe_w_router"].shape[-1]
    w_router = jnp.pad(P["moe_w_router"][j], ((0, 0), (0, LANE - n_exp)))
    n_l = P["moe_w_gate"].shape[0]
    wg = P["moe_w_gate"].reshape((n_l * n_exp,) + P["moe_w_gate"].shape[2:])
    wu = P["moe_w_up"].reshape((n_l * n_exp,) + P["moe_w_up"].shape[2:])
    wd = P["moe_w_down"].reshape((n_l * n_exp,) + P["moe_w_down"].shape[2:])
    hs, gs, ss, counts = [], [], [], []
    for x, _, shift, scale, _, _ in requests:
        h32, gates, selmask = _prep_moe(x, g_row, shift, scale, w_router, n_exp, F32)
        m = x.shape[0] * x.shape[1]
        counts.append(m)
        hs.append(h32.reshape(m, D))
        gs.append(gates.reshape(m, LANE)[:, :n_exp])
        ss.append(selmask.reshape(m, LANE)[:, :n_exp])
    tm = _div_tile(max(counts), 512)
    row_src, tile_expert, row_gate, tok_pos = _route_metadata(
        jnp.concatenate(gs), jnp.concatenate(ss), n_exp, tm)
    a_sorted = _gather_rows(jnp.concatenate(hs), row_src)
    hid = _moe_up_sorted(a_sorted[None], wg, wu, j * n_exp, n_exp, tile_expert, tm)
    y = _moe_down_sorted(hid[None], wd, j * n_exp, n_exp, tile_expert, row_gate, tm)
    outs, start = [], 0
    for (x, _, _, _, gate, _), m in zip(requests, counts):
        outs.append(_moe_combine(y, tok_pos[start * TOP_K:(start + m) * TOP_K], x, gate))
        start += m
    return outs


def _run_trunks(trunks, P):
    requests = [next(t) for t in trunks]
    results = [None] * len(trunks)
    while any(r is None for r in results):
        outs = _moe_joint(requests, P)
        for i, t in enumerate(trunks):
            try:
                requests[i] = t.send(outs[i])
            except StopIteration as stop:
                results[i] = stop.value
    return results


def kernel(x_prompt, x_sample, c_prompt, c_sample, cache_k, cache_v, cache_logf, page_table, state_shift, state_wkv, ada_w, ada_b, norm_g, rwkv_mu, rwkv_w_rkv, rwkv_w0, rwkv_w1, rwkv_w2, rwkv_a0, rwkv_a1, rwkv_a2, rwkv_v0, rwkv_v1, rwkv_v2, rwkv_g1, rwkv_g2, rwkv_k_k, rwkv_k_a, rwkv_r_k, rwkv_ln_w, rwkv_ln_b, rwkv_w_o, attn_w_qkvf, attn_b_f, attn_q_g, attn_k_g, attn_w_o, ffn_w_gate, ffn_w_up, ffn_w_down, moe_w_router, moe_w_gate, moe_w_up, moe_w_down):
    B, T, D = x_prompt.shape
    Bd = x_sample.shape[0]
    depth = ada_w.shape[0]
    n_layers_attn, n_pool, page, n_attn_heads, head_dim = cache_k.shape

    P = dict(
        norm_g=norm_g, rwkv_mu=rwkv_mu, rwkv_w_rkv=rwkv_w_rkv, rwkv_w0=rwkv_w0, rwkv_w1=rwkv_w1,
        rwkv_w2=rwkv_w2, rwkv_a0=rwkv_a0, rwkv_a1=rwkv_a1, rwkv_a2=rwkv_a2, rwkv_v0=rwkv_v0,
        rwkv_v1=rwkv_v1, rwkv_v2=rwkv_v2, rwkv_g1=rwkv_g1, rwkv_g2=rwkv_g2, rwkv_k_k=rwkv_k_k,
        rwkv_k_a=rwkv_k_a, rwkv_r_k=rwkv_r_k, rwkv_ln_w=rwkv_ln_w, rwkv_ln_b=rwkv_ln_b,
        rwkv_w_o=rwkv_w_o, attn_w_qkvf=attn_w_qkvf, attn_b_f=attn_b_f, attn_q_g=attn_q_g,
        attn_k_g=attn_k_g, attn_w_o=attn_w_o, ffn_w_gate=ffn_w_gate, ffn_w_up=ffn_w_up,
        ffn_w_down=ffn_w_down, moe_w_router=moe_w_router, moe_w_gate=moe_w_gate, moe_w_up=moe_w_up,
        moe_w_down=moe_w_down,
        state_dims=(state_wkv.shape[2], state_wkv.shape[3]),
        cache_k=cache_k.reshape(n_layers_attn, n_pool, page, n_attn_heads // SUBLANE, SUBLANE, head_dim),
        cache_v=cache_v.reshape(n_layers_attn, n_pool, page, n_attn_heads // SUBLANE, SUBLANE, head_dim),
        cache_lf=cache_logf,
        page_table=page_table,
    )

    n_rows = B + Bd
    rows_pad = -(-n_rows // SUBLANE) * SUBLANE
    c_all = jnp.pad(jnp.concatenate([c_prompt, c_sample], axis=0), ((0, rows_pad - n_rows), (0, 0)))
    mods_all = _ada_mods(c_all, ada_w.reshape(depth * 2, D, 3 * D), ada_b.reshape(depth * 2, 1, 3 * D))

    def split_mods(r0, r1, per_row):
        out = {}
        for i in range(depth):
            for s in range(2):
                m = mods_all[i * 2 + s, r0:r1]
                parts = [m[:, k * D:(k + 1) * D] for k in range(3)]
                out[(i, s)] = tuple(p[None] if per_row else p[:, None, :] for p in parts)
        return out

    (yp, ks, vs, lfs, shifts, wkvs), (ys, ks2, vs2, lfs2, shifts2, wkvs2) = _run_trunks(
        [_trunk(x_prompt, split_mods(0, B, False), T, True, P, None, None, False),
         _trunk(x_sample.reshape(1, Bd, D), split_mods(B, B + Bd, True), 1, False, P, state_shift,
                state_wkv, True)], P)
    return (yp, ys.reshape(Bd, 1, D),
            jnp.stack(ks), jnp.stack(vs), jnp.stack(lfs),
            jnp.stack(ks2), jnp.stack(vs2), jnp.stack(lfs2),
            jnp.stack(shifts), jnp.stack(wkvs), jnp.stack(shifts2), jnp.stack(wkvs2))
```
